```python
import math
import jax
import jax.numpy as jnp
from jax import lax
import numpy as np

D_MODEL = 1024
BATCH = 16
SEQ = 2048
DEPTH = 4
DEC_BATCH = 32
DEC_SEQ = 32
PAST_LEN = 2048

CHUNK = 64
QBLOCK = 128
D_MIX = D_MODEL
M_HEADS = 8
M_DK = 64
M_DV = 64
M_WIDTH = M_HEADS * M_DV
A_HEADS = 8
NOPE_DIM = 64
ROPE_DIM = 32
V_DIM = 64
A_WIDTH = A_HEADS * V_DIM
Q_RANK = 256
KV_RANK = 256
ROPE_THETA = 10000.0
D_FF = -(-8 * D_MODEL // (3 * 256)) * 256
RMS_EPS = 1e-6
F_BIAS_INIT = 3.0
ATTN_SCALE = (NOPE_DIM + ROPE_DIM) ** -0.5
IN_SIZES = (M_HEADS * M_DK, M_HEADS * M_DK, M_WIDTH, M_WIDTH, 2 * M_HEADS, Q_RANK, KV_RANK, ROPE_DIM)
D_IN = 2 * M_HEADS * M_DK + 2 * M_WIDTH + 2 * M_HEADS + Q_RANK + KV_RANK + ROPE_DIM

kernel_name = "hymba_mlstm_mla_streaming_step"


def rmsnorm(x, g):
    xf = x.astype(jnp.float32)
    y = xf * lax.rsqrt(jnp.mean(xf * xf, axis=-1, keepdims=True) + RMS_EPS)
    return (y * g.astype(jnp.float32)).astype(x.dtype)


def apply_rope(x, pos):
    half = ROPE_DIM // 2
    inv = ROPE_THETA ** (-jnp.arange(half, dtype=jnp.float32) / half)
    ang = pos.astype(jnp.float32)[:, None] * inv[None, :]
    shape = (ang.shape[0],) + (1,) * (x.ndim - 3) + (half,)
    cos = jnp.cos(ang).reshape(shape)
    sin = jnp.sin(ang).reshape(shape)
    x1 = x[..., :half].astype(jnp.float32)
    x2 = x[..., half:].astype(jnp.float32)
    return jnp.concatenate([x1 * cos - x2 * sin, x1 * sin + x2 * cos], axis=-1).astype(x.dtype)


def project_inputs(h, pos, w_in, b_gates, g_q_norm, w_uq, g_kv_norm):
    B, S, _ = h.shape
    idx = [int(i) for i in np.cumsum(IN_SIZES)[:-1]]
    qm, km, vm, om, gates, cq, ckv, kr = jnp.split(h @ w_in, idx, axis=-1)
    qm = qm.reshape(B, S, M_HEADS, M_DK)
    km = km.reshape(B, S, M_HEADS, M_DK) * (M_DK ** -0.5)
    vm = vm.reshape(B, S, M_HEADS, M_DV)
    gates = gates.astype(jnp.float32) + b_gates.astype(jnp.float32)
    ig = gates[..., :M_HEADS]
    lf = jax.nn.log_sigmoid(gates[..., M_HEADS:])
    q = (rmsnorm(cq, g_q_norm) @ w_uq).reshape(B, S, A_HEADS, NOPE_DIM + ROPE_DIM)
    q_nope = q[..., :NOPE_DIM]
    q_rope = apply_rope(q[..., NOPE_DIM:], pos)
    ckv = rmsnorm(ckv, g_kv_norm)
    kr = apply_rope(kr, pos)
    return qm, km, vm, om, ig, lf, q_nope, q_rope, ckv, kr


def expand_kv(ckv, w_ukv):
    B, T, _ = ckv.shape
    kv = (ckv @ w_ukv).reshape(B, T, A_HEADS, NOPE_DIM + V_DIM)
    return kv[..., :NOPE_DIM], kv[..., NOPE_DIM:]


def mlstm_chunk(carry, xs):
    C0, n0, m0 = carry
    q, k, v, ig, lf = xs
    q = q.astype(jnp.float32)
    k = k.astype(jnp.float32)
    v = v.astype(jnp.float32)
    L = q.shape[2]
    b = jnp.cumsum(lf, axis=-1)
    causal = jnp.tril(jnp.ones((L, L), dtype=bool))
    D = jnp.where(causal, b[..., :, None] - b[..., None, :] + ig[..., None, :], -jnp.inf)
    inter = b + m0[..., None]
    m = jnp.maximum(inter, jnp.max(D, axis=-1))
    S = jnp.einsum('bhtd,bhsd->bhts', q, k) * jnp.exp(D - m[..., None])
    w_inter = jnp.exp(inter - m)
    num = w_inter[..., None] * jnp.einsum('bhtd,bhde->bhte', q, C0) + jnp.einsum('bhts,bhse->bhte', S, v)
    den = w_inter * jnp.einsum('bhtd,bhd->bht', q, n0) + jnp.sum(S, axis=-1)
    h = num / jnp.maximum(jnp.abs(den), jnp.exp(-m))[..., None]
    m_new = m[..., -1]
    w = jnp.exp(b[..., -1:] - b + ig - m_new[..., None])
    decay = jnp.exp(b[..., -1] + m0 - m_new)
    C_new = decay[..., None, None] * C0 + jnp.einsum('bhs,bhsd,bhse->bhde', w, k, v)
    n_new = decay[..., None] * n0 + jnp.einsum('bhs,bhsd->bhd', w, k)
    return (C_new, n_new, m_new), h


def mlstm_prompt(qm, km, vm, ig, lf):
    B, S, H, _ = qm.shape
    nc = S // CHUNK

    def blocks4(t):
        return t.reshape(B, nc, CHUNK, H, t.shape[-1]).transpose(1, 0, 3, 2, 4)

    def blocks3(t):
        return t.reshape(B, nc, CHUNK, H).transpose(1, 0, 3, 2)

    init = (jnp.zeros((B, H, M_DK, M_DV), jnp.float32),
            jnp.zeros((B, H, M_DK), jnp.float32),
            jnp.zeros((B, H), jnp.float32))
    carry, hs = lax.scan(mlstm_chunk, init,
                         (blocks4(qm), blocks4(km), blocks4(vm), blocks3(ig), blocks3(lf)))
    hs = hs.transpose(1, 2, 0, 3, 4).reshape(B, H, S, M_DV)
    return carry, hs


def mlstm_sample(state, qm, km, vm, ig, lf):
    return mlstm_chunk(state, (qm.transpose(0, 2, 1, 3), km.transpose(0, 2, 1, 3), vm.transpose(0, 2, 1, 3),
                               ig.transpose(0, 2, 1), lf.transpose(0, 2, 1)))


def mlstm_output(h_cell, om, g_head, dtype):
    B, H, S, _ = h_cell.shape
    h = h_cell.transpose(0, 2, 1, 3)
    h = h * lax.rsqrt(jnp.mean(h * h, axis=-1, keepdims=True) + RMS_EPS)
    h = h * g_head.astype(jnp.float32).reshape(M_HEADS, M_DV)
    o = jax.nn.sigmoid(om.astype(jnp.float32)).reshape(B, S, M_HEADS, M_DV)
    return (o * h).reshape(B, S, M_WIDTH).astype(dtype)


def mla_attend(qn, qr, kn, kr, v, mask):
    s = (jnp.einsum('bqhd,bkhd->bhqk', qn, kn) + jnp.einsum('bqhd,bkd->bhqk', qr, kr)).astype(jnp.float32)
    s = s * ATTN_SCALE
    if mask is not None:
        s = jnp.where(mask, s, -jnp.inf)
    p = jax.nn.softmax(s, axis=-1).astype(v.dtype)
    return jnp.einsum('bhqk,bkhd->bqhd', p, v)


def mla_prompt_attention(q_nope, q_rope, k_nope, k_rope, v):
    B, S, H, _ = q_nope.shape
    nb = S // QBLOCK
    qn = q_nope.reshape(B, nb, QBLOCK, H, NOPE_DIM).transpose(1, 0, 2, 3, 4)
    qr = q_rope.reshape(B, nb, QBLOCK, H, ROPE_DIM).transpose(1, 0, 2, 3, 4)
    kpos = jnp.arange(S)

    def block(args):
        i, qn_b, qr_b = args
        qpos = i * QBLOCK + jnp.arange(QBLOCK)
        reach = (qpos // CHUNK + 1) * CHUNK
        mask = kpos[None, :] < reach[:, None]
        return mla_attend(qn_b, qr_b, k_nope, k_rope, v, mask)

    out = lax.map(block, (jnp.arange(nb), qn, qr))
    return out.transpose(1, 0, 2, 3, 4).reshape(B, S, A_WIDTH)


def residual_update(x, h_m, attn, w_out, g_post_mix, g_pre_ffn, w_ffn_in, w_ffn_out, g_post_ffn):
    x = x + rmsnorm(jnp.concatenate([h_m, attn], axis=-1) @ w_out, g_post_mix)
    h = rmsnorm(x, g_pre_ffn)
    gate, up = jnp.split(h @ w_ffn_in, 2, axis=-1)
    return x + rmsnorm((jax.nn.silu(gate) * up) @ w_ffn_out, g_post_ffn)


def setup_inputs(seed: int = 0) -> dict:
    key = jax.random.key(seed)
    ks = jax.random.split(key, 24)
    f32 = jnp.float32

    def nrm(k, shape, scale):
        return jax.random.normal(k, shape, f32) * scale

    def gain(k, shape):
        return 1.0 + 0.05 * jax.random.normal(k, shape, f32)

    b_gates = jnp.concatenate([nrm(ks[8], (DEPTH, M_HEADS), 0.1),
                               F_BIAS_INIT + nrm(ks[9], (DEPTH, M_HEADS), 0.5)], axis=-1)
    return {
        "x_prompt": nrm(ks[0], (BATCH, SEQ, D_MODEL), 1.0),
        "x_sample": nrm(ks[1], (DEC_BATCH, DEC_SEQ, D_MODEL), 1.0),
        "cache_ckv": nrm(ks[2], (DEPTH, DEC_BATCH, PAST_LEN, KV_RANK), 1.0),
        "cache_krope": nrm(ks[3], (DEPTH, DEC_BATCH, PAST_LEN, ROPE_DIM), 1.0),
        "state_mlstm_C": nrm(ks[4], (DEPTH, DEC_BATCH, M_HEADS, M_DK, M_DV), 0.5),
        "state_mlstm_n": nrm(ks[5], (DEPTH, DEC_BATCH, M_HEADS, M_DK), 0.5),
        "state_mlstm_m": nrm(ks[6], (DEPTH, DEC_BATCH, M_HEADS), 1.0),
        "g_pre_mix": gain(ks[10], (DEPTH, D_MODEL)),
        "g_post_mix": gain(ks[11], (DEPTH, D_MODEL)),
        "g_pre_ffn": gain(ks[12], (DEPTH, D_MODEL)),
        "g_post_ffn": gain(ks[13], (DEPTH, D_MODEL)),
        "w_in": nrm(ks[7], (DEPTH, D_MODEL, D_IN), D_MODEL ** -0.5),
        "b_gates": b_gates,
        "g_mlstm_head": gain(ks[14], (DEPTH, M_WIDTH)),
        "g_q_norm": gain(ks[15], (DEPTH, Q_RANK)),
        "w_uq": nrm(ks[16], (DEPTH, Q_RANK, A_HEADS * (NOPE_DIM + ROPE_DIM)), Q_RANK ** -0.5),
        "g_kv_norm": gain(ks[17], (DEPTH, KV_RANK)),
        "w_ukv": nrm(ks[18], (DEPTH, KV_RANK, A_HEADS * (NOPE_DIM + V_DIM)), KV_RANK ** -0.5),
        "w_out": nrm(ks[19], (DEPTH, D_MIX, D_MODEL), D_MIX ** -0.5),
        "w_ffn_in": nrm(ks[20], (DEPTH, D_MODEL, 2 * D_FF), D_MODEL ** -0.5),
        "w_ffn_out": nrm(ks[21], (DEPTH, D_FF, D_MODEL), D_FF ** -0.5),
    }


def reference(x_prompt, x_sample, cache_ckv, cache_krope, state_mlstm_C, state_mlstm_n, state_mlstm_m,
              g_pre_mix, g_post_mix, g_pre_ffn, g_post_ffn, w_in, b_gates, g_mlstm_head,
              g_q_norm, w_uq, g_kv_norm, w_ukv, w_out, w_ffn_in, w_ffn_out):
    pos_p = jnp.arange(x_prompt.shape[1])
    pos_s = cache_ckv.shape[2] + jnp.arange(x_sample.shape[1])
    xp, xs = x_prompt, x_sample
    ckv_p, kr_p, C_p, n_p, m_p = [], [], [], [], []
    ckv_s, kr_s, C_s, n_s, m_s = [], [], [], [], []
    for l in range(DEPTH):
        h = rmsnorm(xp, g_pre_mix[l])
        qm, km, vm, om, ig, lf, qn, qr, ckv, kr = project_inputs(
            h, pos_p, w_in[l], b_gates[l], g_q_norm[l], w_uq[l], g_kv_norm[l])
        (C, n, m), hcell = mlstm_prompt(qm, km, vm, ig, lf)
        kn, v = expand_kv(ckv, w_ukv[l])
        attn = mla_prompt_attention(qn, qr, kn, kr, v)
        xp = residual_update(xp, mlstm_output(hcell, om, g_mlstm_head[l], xp.dtype), attn, w_out[l],
                             g_post_mix[l], g_pre_ffn[l], w_ffn_in[l], w_ffn_out[l], g_post_ffn[l])
        ckv_p.append(ckv)
        kr_p.append(kr)
        C_p.append(C)
        n_p.append(n)
        m_p.append(m)
        h = rmsnorm(xs, g_pre_mix[l])
        qm, km, vm, om, ig, lf, qn, qr, ckv, kr = project_inputs(
            h, pos_s, w_in[l], b_gates[l], g_q_norm[l], w_uq[l], g_kv_norm[l])
        state = (state_mlstm_C[l].astype(jnp.float32), state_mlstm_n[l].astype(jnp.float32),
                 state_mlstm_m[l].astype(jnp.float32))
        (C, n, m), hcell = mlstm_sample(state, qm, km, vm, ig, lf)
        ckv_all = jnp.concatenate([cache_ckv[l], ckv], axis=1)
        kr_all = jnp.concatenate([cache_krope[l], kr], axis=1)
        kn, v = expand_kv(ckv_all, w_ukv[l])
        Bs, T = xs.shape[0], xs.shape[1]
        attn = mla_attend(qn, qr, kn, kr_all, v, None).reshape(Bs, T, A_WIDTH)
        xs = residual_update(xs, mlstm_output(hcell, om, g_mlstm_head[l], xs.dtype), attn, w_out[l],
                             g_post_mix[l], g_pre_ffn[l], w_ffn_in[l], w_ffn_out[l], g_post_ffn[l])
        ckv_s.append(ckv)
        kr_s.append(kr)
        C_s.append(C)
        n_s.append(n)
        m_s.append(m)
    return (xp, xs,
            jnp.stack(ckv_p), jnp.stack(kr_p), jnp.stack(C_p), jnp.stack(n_p), jnp.stack(m_p),
            jnp.stack(ckv_s), jnp.stack(kr_s), jnp.stack(C_s), jnp.stack(n_s), jnp.stack(m_s))
```

```python
import functools

import jax
import jax.numpy as jnp
from jax import lax
from jax.experimental import pallas as pl
from jax.experimental.pallas import tpu as pltpu

F32 = jnp.float32
BF16 = jnp.bfloat16

D_MODEL = 1024
DEPTH = 4
CHUNK = 64
M_HEADS = 8
M_DK = 64
M_DV = 64
M_WIDTH = M_HEADS * M_DV
A_HEADS = 8
NOPE_DIM = 64
ROPE_DIM = 32
ROPE_HALF = ROPE_DIM // 2
V_DIM = 64
A_WIDTH = A_HEADS * V_DIM
Q_RANK = 256
KV_RANK = 256
ROPE_THETA = 10000.0
D_FF = 2816
RMS_EPS = 1e-6
ATTN_SCALE = (NOPE_DIM + ROPE_DIM) ** -0.5

LANES = 128
HEAD_SLOT = LANES
QK_WIDTH = A_HEADS * HEAD_SLOT
ROPE_LANE0 = NOPE_DIM

COL_QM = 0
COL_KM = COL_QM + M_HEADS * M_DK
COL_VM = COL_KM + M_HEADS * M_DK
COL_OM = COL_VM + M_WIDTH
COL_CQ = COL_OM + M_WIDTH
COL_CKV = COL_CQ + Q_RANK
COL_KR = COL_CKV + KV_RANK
COL_G = COL_KR + LANES
D_IN_PAD = COL_G + LANES

FF_CHUNKS = ((0, 768), (768, 1792), (1792, 2816))

VMEM_LIMIT = 56 * 1024 * 1024

NT_DIMS = (((1,), (1,)), ((), ()))
TN_DIMS = (((0,), (0,)), ((), ()))


def _resident(block_shape, index):
    return pl.BlockSpec(block_shape, lambda *_: index, pipeline_mode=pl.Buffered(1))


def _rms(x, g):
    return x * lax.rsqrt(jnp.mean(x * x, axis=-1, keepdims=True) + RMS_EPS) * g


def _rope(x, ra, rbm, rbp):
    return x * ra + pltpu.roll(x, LANES - ROPE_HALF, axis=1) * rbm + pltpu.roll(x, ROPE_HALF, axis=1) * rbp


def _pre_mixer_kernel(x_ref, g_ref, win_ref, bg_ref, gq_ref, wuq_ref, gkv_ref, wukv_ref,
                      ra_ref, rbm_ref, rbp_ref,
                      qm_ref, km_ref, vm_ref, om_ref, gate_ref, qa_ref, ka_ref, va_ref, ckv_ref, kr_ref):
    xb = _rms(x_ref[...], g_ref[...]).astype(BF16)

    def proj(col, width):
        return jnp.dot(xb, win_ref[:, col:col + width], preferred_element_type=F32)

    qm_ref[...] = proj(COL_QM, M_HEADS * M_DK).astype(BF16)
    km_ref[...] = (proj(COL_KM, M_HEADS * M_DK) * (M_DK ** -0.5)).astype(BF16)
    vm_ref[...] = proj(COL_VM, M_WIDTH).astype(BF16)
    om_ref[...] = proj(COL_OM, M_WIDTH)

    gates = proj(COL_G, LANES) + bg_ref[...]
    lane = lax.broadcasted_iota(jnp.int32, gates.shape, 1)
    log_f = jnp.minimum(gates, 0.0) - jnp.log1p(jnp.exp(-jnp.abs(gates)))
    gate_ref[...] = jnp.where(lane < M_HEADS, gates, log_f)

    ra, rbm, rbp = ra_ref[...], rbm_ref[...], rbp_ref[...]

    cqn = _rms(proj(COL_CQ, Q_RANK), gq_ref[...]).astype(BF16)
    q = jnp.dot(cqn, wuq_ref[...], preferred_element_type=F32)
    for h in range(A_HEADS):
        sl = slice(h * HEAD_SLOT, (h + 1) * HEAD_SLOT)
        qa_ref[:, sl] = (_rope(q[:, sl], ra, rbm, rbp) * ATTN_SCALE).astype(BF16)

    ckvn = _rms(proj(COL_CKV, KV_RANK), gkv_ref[...])
    ckv_ref[...] = ckvn
    ckvb = ckvn.astype(BF16)
    kr = _rope(proj(COL_KR, LANES), ra, rbm, rbp)
    kr_ref[...] = kr[:, ROPE_LANE0:ROPE_LANE0 + ROPE_DIM]
    kmat = jnp.dot(ckvb, wukv_ref[:, :QK_WIDTH], preferred_element_type=F32)
    for h in range(A_HEADS):
        sl = slice(h * HEAD_SLOT, (h + 1) * HEAD_SLOT)
        ka_ref[:, sl] = (kmat[:, sl] + kr).astype(BF16)
    va_ref[...] = jnp.dot(ckvb, wukv_ref[:, QK_WIDTH:], preferred_element_type=F32).astype(BF16)


def _pre_mixer(x, layer, prm, rope_tabs, tm, n_rope_blocks, name):
    T = x.shape[0]
    ra, rbm, rbp = rope_tabs
    row = lambda w: pl.BlockSpec((tm, w), lambda i: (i, 0))
    rope_spec = pl.BlockSpec((tm, LANES), lambda i: (i % n_rope_blocks, 0))
    lyr = lambda shape: _resident((None,) + shape, (layer, 0, 0))
    out_widths = (M_HEADS * M_DK, M_HEADS * M_DK, M_WIDTH, M_WIDTH, LANES, QK_WIDTH, QK_WIDTH, A_WIDTH,
                  KV_RANK, ROPE_DIM)
    out_dtypes = (BF16, BF16, BF16, F32, F32, BF16, BF16, BF16, F32, F32)
    return pl.pallas_call(
        _pre_mixer_kernel,
        grid=(T // tm,),
        in_specs=[row(D_MODEL), lyr((1, D_MODEL)), lyr((D_MODEL, D_IN_PAD)), lyr((1, LANES)),
                  lyr((1, Q_RANK)), lyr((Q_RANK, QK_WIDTH)), lyr((1, KV_RANK)),
                  lyr((KV_RANK, QK_WIDTH + A_WIDTH)), rope_spec, rope_spec, rope_spec],
        out_specs=[row(w) for w in out_widths],
        out_shape=[jax.ShapeDtypeStruct((T, w), d) for w, d in zip(out_widths, out_dtypes)],
        compiler_params=pltpu.CompilerParams(dimension_semantics=("arbitrary",), vmem_limit_bytes=VMEM_LIMIT),
        name=name,
    )(x, prm["g_pre_mix"], prm["w_in"], prm["b_gates"], prm["g_q_norm"], prm["w_uq"], prm["g_kv_norm"],
      prm["w_ukv"], ra, rbm, rbp)


def _mlstm_kernel(q_ref, k_ref, v_ref, g_ref, c0_ref, n0_ref, m0_ref, h_ref, c_ref, n_ref, m_ref, *, L):
    @pl.when(pl.program_id(1) == 0)
    def _():
        c_ref[...] = c0_ref[...]
        n_ref[...] = n0_ref[...]
        m_ref[...] = m0_ref[...]

    gates = g_ref[...]
    t_idx = lax.broadcasted_iota(jnp.int32, (L, L), 0)
    s_idx = lax.broadcasted_iota(jnp.int32, (L, L), 1)
    causal = s_idx <= t_idx
    csum = jnp.dot(causal.astype(F32), gates, precision=lax.Precision.HIGHEST, preferred_element_type=F32)
    b_col = pltpu.roll(csum, LANES - M_HEADS, axis=1)
    a_col = gates - b_col
    sel = (lax.broadcasted_iota(jnp.int32, (M_HEADS, LANES), 0)
           == lax.broadcasted_iota(jnp.int32, (M_HEADS, LANES), 1)).astype(F32)
    a_rows = lax.dot_general(sel, a_col, NT_DIMS, precision=lax.Precision.HIGHEST,
                             preferred_element_type=F32)

    for h in range(M_HEADS):
        sl = slice(h * M_DK, (h + 1) * M_DK)
        q, k, v = q_ref[:, sl], k_ref[:, sl], v_ref[:, sl]
        m0 = m_ref[h:h + 1, 0:1]
        c0 = c_ref[h]
        n0 = n_ref[h:h + 1, :]
        d_m = jnp.where(causal, a_rows[h:h + 1, :], -jnp.inf)
        mx = jnp.maximum(jnp.max(d_m, axis=-1, keepdims=True), m0)
        s_mat = lax.dot_general(q, k, NT_DIMS, preferred_element_type=F32) * jnp.exp(d_m - mx)
        w_inter = jnp.exp(m0 - mx)
        num = (w_inter * jnp.dot(q, c0.astype(BF16), preferred_element_type=F32)
               + jnp.dot(s_mat.astype(BF16), v, preferred_element_type=F32))
        den = (w_inter * jnp.sum(q.astype(F32) * n0, axis=-1, keepdims=True)
               + jnp.sum(s_mat, axis=-1, keepdims=True))
        b_h = b_col[:, h:h + 1]
        h_cell = num / jnp.maximum(jnp.abs(den), jnp.exp(-(b_h + mx)))
        h_ref[:, sl] = h_cell * lax.rsqrt(jnp.mean(h_cell * h_cell, axis=-1, keepdims=True) + RMS_EPS)

        mx_last = mx[L - 1:L, :]
        decay = jnp.exp(m0 - mx_last)
        kw = k.astype(F32) * jnp.exp(a_col[:, h:h + 1] - mx_last)
        c_ref[h] = decay * c0 + lax.dot_general(kw.astype(BF16), v, TN_DIMS, preferred_element_type=F32)
        n_ref[h:h + 1, :] = decay * n0 + jnp.sum(kw, axis=0, keepdims=True)
        m_ref[h:h + 1, :] = jnp.broadcast_to(b_h[L - 1:L, :] + mx_last, (1, LANES))


def _mlstm(qm, km, vm, gates, c0, n0, m0, n_streams, n_chunks, L, name):
    T = qm.shape[0]
    tok = lambda w: pl.BlockSpec((L, w), lambda b, c: (b * n_chunks + c, 0))
    st_c = pl.BlockSpec((None, M_HEADS, M_DK, M_DV), lambda b, c: (b, 0, 0, 0))
    st_n = pl.BlockSpec((None, M_HEADS, M_DK), lambda b, c: (b, 0, 0))
    st_m = pl.BlockSpec((None, M_HEADS, LANES), lambda b, c: (b, 0, 0))
    return pl.pallas_call(
        functools.partial(_mlstm_kernel, L=L),
        grid=(n_streams, n_chunks),
        in_specs=[tok(M_HEADS * M_DK), tok(M_HEADS * M_DK), tok(M_WIDTH), tok(LANES), st_c, st_n, st_m],
        out_specs=[tok(M_WIDTH), st_c, st_n, st_m],
        out_shape=[jax.ShapeDtypeStruct((T, M_WIDTH), F32),
                   jax.ShapeDtypeStruct((n_streams, M_HEADS, M_DK, M_DV), F32),
                   jax.ShapeDtypeStruct((n_streams, M_HEADS, M_DK), F32),
                   jax.ShapeDtypeStruct((n_streams, M_HEADS, LANES), F32)],
        compiler_params=pltpu.CompilerParams(dimension_semantics=("arbitrary", "arbitrary")),
        name=name,
    )(qm, km, vm, gates, c0, n0, m0)


def _softmax_step(carry, s, v):
    m, l, acc = carry
    m_new = jnp.maximum(m, jnp.max(s, axis=-1, keepdims=True))
    alpha = jnp.exp(m - m_new)
    p = jnp.exp(s - m_new)
    l = alpha * l + jnp.sum(p, axis=-1, keepdims=True)
    acc = alpha * acc + jnp.dot(p.astype(BF16), v, preferred_element_type=F32)
    return m_new, l, acc


def _softmax_init(rows):
    return (jnp.full((rows, 1), -jnp.inf, F32), jnp.zeros((rows, 1), F32), jnp.zeros((rows, LANES), F32))


def _attn_prompt_kernel(q_ref, k_ref, v_ref, o_ref, *, tq):
    i = pl.program_id(1)
    row = lax.broadcasted_iota(jnp.int32, (tq, tq), 0)
    col = lax.broadcasted_iota(jnp.int32, (tq, tq), 1)
    diag_mask = (col // CHUNK) <= (row // CHUNK)
    lane = lax.broadcasted_iota(jnp.int32, (tq, LANES), 1)
    for pair in range(A_HEADS // 2):
        vsl = slice(pair * LANES, (pair + 1) * LANES)
        outs = []
        for h in (2 * pair, 2 * pair + 1):
            hsl = slice(h * HEAD_SLOT, (h + 1) * HEAD_SLOT)
            q = q_ref[:, hsl]

            def scores(j):
                rows = pl.ds(pl.multiple_of(j * tq, tq), tq)
                s = lax.dot_general(q, k_ref[rows, hsl], NT_DIMS, preferred_element_type=F32)
                return s, v_ref[rows, vsl]

            def body(j, carry):
                s, v = scores(j)
                return _softmax_step(carry, s, v)

            carry = lax.fori_loop(0, i, body, _softmax_init(tq))
            s, v = scores(i)
            _, l, acc = _softmax_step(carry, jnp.where(diag_mask, s, -jnp.inf), v)
            outs.append(acc / l)
        o_ref[:, vsl] = jnp.where(lane < V_DIM, outs[0], outs[1]).astype(BF16)


def _attn_prompt(qa, ka, va, n_streams, seq, tq, name):
    T = qa.shape[0]
    nq = seq // tq
    return pl.pallas_call(
        functools.partial(_attn_prompt_kernel, tq=tq),
        grid=(n_streams, nq),
        in_specs=[pl.BlockSpec((tq, QK_WIDTH), lambda b, i: (b * nq + i, 0)),
                  pl.BlockSpec((seq, QK_WIDTH), lambda b, i: (b, 0)),
                  pl.BlockSpec((seq, A_WIDTH), lambda b, i: (b, 0))],
        out_specs=pl.BlockSpec((tq, A_WIDTH), lambda b, i: (b * nq + i, 0)),
        out_shape=jax.ShapeDtypeStruct((T, A_WIDTH), BF16),
        compiler_params=pltpu.CompilerParams(dimension_semantics=("arbitrary", "arbitrary"),
                                             vmem_limit_bytes=VMEM_LIMIT),
        name=name,
    )(qa, ka, va)


def _attn_sample_kernel(q_ref, kn_ref, vn_ref, ckv_ref, kr_ref, wukv_ref, place_ref, o_ref, kp_ref, vp_ref):
    ckvb = ckv_ref[...].astype(BF16)
    kp_ref[...] = (jnp.dot(ckvb, wukv_ref[:, :QK_WIDTH], preferred_element_type=F32)
                   + jnp.dot(kr_ref[...].astype(BF16), place_ref[...], preferred_element_type=F32)).astype(BF16)
    vp_ref[...] = jnp.dot(ckvb, wukv_ref[:, QK_WIDTH:], preferred_element_type=F32).astype(BF16)
    rows = q_ref.shape[0]
    lane = lax.broadcasted_iota(jnp.int32, (rows, LANES), 1)
    for pair in range(A_HEADS // 2):
        vsl = slice(pair * LANES, (pair + 1) * LANES)
        outs = []
        for h in (2 * pair, 2 * pair + 1):
            hsl = slice(h * HEAD_SLOT, (h + 1) * HEAD_SLOT)
            q = q_ref[:, hsl]
            s_past = lax.dot_general(q, kp_ref[:, hsl], NT_DIMS, preferred_element_type=F32)
            s_new = lax.dot_general(q, kn_ref[:, hsl], NT_DIMS, preferred_element_type=F32)
            carry = _softmax_step(_softmax_init(rows), s_past, vp_ref[:, vsl])
            _, l, acc = _softmax_step(carry, s_new, vn_ref[:, vsl])
            outs.append(acc / l)
        o_ref[:, vsl] = jnp.where(lane < V_DIM, outs[0], outs[1]).astype(BF16)


def _attn_sample(qa, ka, va, cache_ckv, cache_krope, layer, prm, place, name):
    T = qa.shape[0]
    n_streams, past = cache_ckv.shape[1], cache_ckv.shape[2]
    L = T // n_streams
    tok = lambda w: pl.BlockSpec((L, w), lambda b: (b, 0))
    return pl.pallas_call(
        _attn_sample_kernel,
        grid=(n_streams,),
        in_specs=[tok(QK_WIDTH), tok(QK_WIDTH), tok(A_WIDTH),
                  pl.BlockSpec((None, None, past, KV_RANK), lambda b: (layer, b, 0, 0)),
                  pl.BlockSpec((None, None, past, ROPE_DIM), lambda b: (layer, b, 0, 0)),
                  _resident((None, KV_RANK, QK_WIDTH + A_WIDTH), (layer, 0, 0)),
                  _resident((ROPE_DIM, QK_WIDTH), (0, 0))],
        out_specs=tok(A_WIDTH),
        out_shape=jax.ShapeDtypeStruct((T, A_WIDTH), BF16),
        scratch_shapes=[pltpu.VMEM((past, QK_WIDTH), BF16), pltpu.VMEM((past, A_WIDTH), BF16)],
        compiler_params=pltpu.CompilerParams(dimension_semantics=("arbitrary",), vmem_limit_bytes=VMEM_LIMIT),
        name=name,
    )(qa, ka, va, cache_ckv, cache_krope, prm["w_ukv"], place)


def _post_mixer_kernel(x_ref, hn_ref, om_ref, at_ref, gh_ref, wo_ref, gpm_ref, gpf_ref, w1_ref, w2_ref, gpo_ref,
                       o_ref):
    h_m = jax.nn.sigmoid(om_ref[...]) * (hn_ref[...] * gh_ref[...])
    mix = jnp.dot(jnp.concatenate([h_m.astype(BF16), at_ref[...]], axis=-1), wo_ref[...],
                  preferred_element_type=F32)
    x1 = x_ref[...] + _rms(mix, gpm_ref[...])
    hb = _rms(x1, gpf_ref[...]).astype(BF16)
    acc = None
    for lo, hi in FF_CHUNKS:
        gate = jnp.dot(hb, w1_ref[:, lo:hi], preferred_element_type=F32)
        up = jnp.dot(hb, w1_ref[:, D_FF + lo:D_FF + hi], preferred_element_type=F32)
        act = (gate * jax.nn.sigmoid(gate) * up).astype(BF16)
        part = jnp.dot(act, w2_ref[lo:hi, :], preferred_element_type=F32)
        acc = part if acc is None else acc + part
    o_ref[...] = x1 + _rms(acc, gpo_ref[...])


def _post_mixer(x, hn, om, attn, layer, prm, tm, name):
    T = x.shape[0]
    row = lambda w: pl.BlockSpec((tm, w), lambda i: (i, 0))
    lyr = lambda shape: _resident((None,) + shape, (layer, 0, 0))
    return pl.pallas_call(
        _post_mixer_kernel,
        grid=(T // tm,),
        in_specs=[row(D_MODEL), row(M_WIDTH), row(M_WIDTH), row(A_WIDTH), lyr((1, M_WIDTH)),
                  lyr((D_MODEL, D_MODEL)), lyr((1, D_MODEL)), lyr((1, D_MODEL)), lyr((D_MODEL, 2 * D_FF)),
                  lyr((D_FF, D_MODEL)), lyr((1, D_MODEL))],
        out_specs=row(D_MODEL),
        out_shape=jax.ShapeDtypeStruct((T, D_MODEL), F32),
        compiler_params=pltpu.CompilerParams(dimension_semantics=("arbitrary",), vmem_limit_bytes=VMEM_LIMIT),
        name=name,
    )(x, hn, om, attn, prm["g_mlstm_head"], prm["w_out"], prm["g_post_mix"], prm["g_pre_ffn"], prm["w_ffn_in"],
      prm["w_ffn_out"], prm["g_post_ffn"])


def _prepare_params(g_pre_mix, g_post_mix, g_pre_ffn, g_post_ffn, w_in, b_gates, g_mlstm_head, g_q_norm, w_uq,
                    g_kv_norm, w_ukv, w_out, w_ffn_in, w_ffn_out):
    depth = w_in.shape[0]
    sizes = (M_HEADS * M_DK, M_HEADS * M_DK, M_WIDTH, M_WIDTH, 2 * M_HEADS, Q_RANK, KV_RANK, ROPE_DIM)
    offs = [0]
    for s in sizes:
        offs.append(offs[-1] + s)
    part = lambda i: w_in[:, :, offs[i]:offs[i + 1]]
    zeros = lambda w: jnp.zeros((depth, D_MODEL, w), w_in.dtype)
    w_in_r = jnp.concatenate(
        [part(0), part(1), part(2), part(3), part(5), part(6),
         zeros(ROPE_LANE0), part(7), zeros(LANES - ROPE_LANE0 - ROPE_DIM),
         part(4), zeros(LANES - 2 * M_HEADS)], axis=-1).astype(BF16)
    w_uq_r = jnp.pad(w_uq.reshape(depth, Q_RANK, A_HEADS, NOPE_DIM + ROPE_DIM),
                     ((0, 0), (0, 0), (0, 0), (0, HEAD_SLOT - NOPE_DIM - ROPE_DIM)))
    w_uq_r = w_uq_r.reshape(depth, Q_RANK, QK_WIDTH).astype(BF16)
    w_ukv4 = w_ukv.reshape(depth, KV_RANK, A_HEADS, NOPE_DIM + V_DIM)
    w_uk_r = jnp.pad(w_ukv4[..., :NOPE_DIM], ((0, 0), (0, 0), (0, 0), (0, HEAD_SLOT - NOPE_DIM)))
    w_ukv_r = jnp.concatenate([w_uk_r.reshape(depth, KV_RANK, QK_WIDTH),
                               w_ukv4[..., NOPE_DIM:].reshape(depth, KV_RANK, A_WIDTH)], axis=-1).astype(BF16)
    row = lambda g: g.astype(F32)[:, None, :]
    return {
        "g_pre_mix": row(g_pre_mix), "g_post_mix": row(g_post_mix), "g_pre_ffn": row(g_pre_ffn),
        "g_post_ffn": row(g_post_ffn), "g_mlstm_head": row(g_mlstm_head), "g_q_norm": row(g_q_norm),
        "g_kv_norm": row(g_kv_norm),
        "b_gates": jnp.pad(b_gates.astype(F32), ((0, 0), (0, LANES - 2 * M_HEADS)))[:, None, :],
        "w_in": w_in_r, "w_uq": w_uq_r, "w_ukv": w_ukv_r,
        "w_out": w_out.astype(BF16), "w_ffn_in": w_ffn_in.astype(BF16), "w_ffn_out": w_ffn_out.astype(BF16),
    }


def _rope_tables(pos):
    inv = ROPE_THETA ** (-jnp.arange(ROPE_HALF, dtype=F32) / ROPE_HALF)
    ang = pos.astype(F32)[:, None] * inv[None, :]
    cos, sin = jnp.cos(ang), jnp.sin(ang)
    n = pos.shape[0]
    z = lambda w: jnp.zeros((n, w), F32)
    tail = LANES - ROPE_LANE0 - ROPE_DIM
    ra = jnp.concatenate([jnp.ones((n, ROPE_LANE0), F32), cos, cos, z(tail)], axis=-1)
    rbm = jnp.concatenate([z(ROPE_LANE0), -sin, z(ROPE_HALF), z(tail)], axis=-1)
    rbp = jnp.concatenate([z(ROPE_LANE0), z(ROPE_HALF), sin, z(tail)], axis=-1)
    return ra, rbm, rbp


def _krope_placement():
    r = lax.broadcasted_iota(jnp.int32, (ROPE_DIM, QK_WIDTH), 0)
    c = lax.broadcasted_iota(jnp.int32, (ROPE_DIM, QK_WIDTH), 1)
    return ((c % HEAD_SLOT) == (r + ROPE_LANE0)).astype(BF16)


TM_PROMPT = 512
TM_SAMPLE = 256
TQ = 256


def kernel(x_prompt, x_sample, cache_ckv, cache_krope, state_mlstm_C, state_mlstm_n, state_mlstm_m,
           g_pre_mix, g_post_mix, g_pre_ffn, g_post_ffn, w_in, b_gates, g_mlstm_head,
           g_q_norm, w_uq, g_kv_norm, w_ukv, w_out, w_ffn_in, w_ffn_out):
    B, S, _ = x_prompt.shape
    Bs, Ls, _ = x_sample.shape
    depth, _, past, _ = cache_ckv.shape
    assert S % TM_PROMPT == 0 and S % TQ == 0 and S % CHUNK == 0 and TM_SAMPLE % Ls == 0
    assert (Bs * Ls) % TM_SAMPLE == 0

    prm = _prepare_params(g_pre_mix, g_post_mix, g_pre_ffn, g_post_ffn, w_in, b_gates, g_mlstm_head, g_q_norm,
                          w_uq, g_kv_norm, w_ukv, w_out, w_ffn_in, w_ffn_out)
    rope_p = _rope_tables(jnp.arange(S))
    rope_s = _rope_tables(past + jnp.arange(TM_SAMPLE) % Ls)
    place = _krope_placement()

    xp = x_prompt.reshape(B * S, D_MODEL)
    xs = x_sample.reshape(Bs * Ls, D_MODEL)
    zero_c = jnp.zeros((B, M_HEADS, M_DK, M_DV), F32)
    zero_n = jnp.zeros((B, M_HEADS, M_DK), F32)
    zero_m = jnp.zeros((B, M_HEADS, LANES), F32)
    m_in = jnp.broadcast_to(state_mlstm_m.astype(F32)[..., None], (depth, Bs, M_HEADS, LANES))

    outs_p = {k: [] for k in ("ckv", "kr", "C", "n", "m")}
    outs_s = {k: [] for k in ("ckv", "kr", "C", "n", "m")}
    for l in range(depth):
        qm, km, vm, om, gates, qa, ka, va, ckv, kr = _pre_mixer(
            xp, l, prm, rope_p, TM_PROMPT, S // TM_PROMPT, f"pre_mixer_p{l}")
        hn, C, n, m = _mlstm(qm, km, vm, gates, zero_c, zero_n, zero_m, B, S // CHUNK, CHUNK, f"mlstm_p{l}")
        attn = _attn_prompt(qa, ka, va, B, S, TQ, f"attn_p{l}")
        xp = _post_mixer(xp, hn, om, attn, l, prm, TM_PROMPT, f"post_mixer_p{l}")
        for key, val in zip(("ckv", "kr", "C", "n", "m"), (ckv, kr, C, n, m)):
            outs_p[key].append(val)
        qm, km, vm, om, gates, qa, ka, va, ckv, kr = _pre_mixer(xs, l, prm, rope_s, TM_SAMPLE, 1, f"pre_mixer_s{l}")
        hn, C, n, m = _mlstm(qm, km, vm, gates, state_mlstm_C[l].astype(F32), state_mlstm_n[l].astype(F32),
                             m_in[l], Bs, 1, Ls, f"mlstm_s{l}")
        attn = _attn_sample(qa, ka, va, cache_ckv, cache_krope, l, prm, place, f"attn_s{l}")
        xs = _post_mixer(xs, hn, om, attn, l, prm, TM_SAMPLE, f"post_mixer_s{l}")
        for key, val in zip(("ckv", "kr", "C", "n", "m"), (ckv, kr, C, n, m)):
            outs_s[key].append(val)

    def collect(o, nb, ln):
        return (jnp.stack(o["ckv"]).reshape(depth, nb, ln, KV_RANK),
                jnp.stack(o["kr"]).reshape(depth, nb, ln, ROPE_DIM),
                jnp.stack(o["C"]), jnp.stack(o["n"]), jnp.stack(o["m"])[..., 0])

    return ((xp.reshape(B, S, D_MODEL), xs.reshape(Bs, Ls, D_MODEL))
            + collect(outs_p, B, S) + collect(outs_s, Bs, Ls))
```

```python
import functools

import jax
import jax.numpy as jnp
from jax import lax
from jax.experimental import pallas as pl
from jax.experimental.pallas import tpu as pltpu

F32 = jnp.float32
BF16 = jnp.bfloat16

D_MODEL = 1024
DEPTH = 4
CHUNK = 64
M_HEADS = 8
M_DK = 64
M_DV = 64
M_WIDTH = M_HEADS * M_DV
A_HEADS = 8
NOPE_DIM = 64
ROPE_DIM = 32
ROPE_HALF = ROPE_DIM // 2
V_DIM = 64
A_WIDTH = A_HEADS * V_DIM
Q_RANK = 256
KV_RANK = 256
ROPE_THETA = 10000.0
D_FF = 2816
RMS_EPS = 1e-6
ATTN_SCALE = (NOPE_DIM + ROPE_DIM) ** -0.5

LANES = 128
HEAD_SLOT = LANES
QK_WIDTH = A_HEADS * HEAD_SLOT
V_SLOT = LANES
VT_ROWS = A_HEADS * V_SLOT
ACC_ROWS = V_DIM + 16
LOG2E = 1.4426950408889634
ROPE_LANE0 = NOPE_DIM

COL_QM = 0
COL_KM = COL_QM + M_HEADS * M_DK
COL_VM = COL_KM + M_HEADS * M_DK
COL_OM = COL_VM + M_WIDTH
COL_CQ = COL_OM + M_WIDTH
COL_CKV = COL_CQ + Q_RANK
COL_KR = COL_CKV + KV_RANK
COL_IG = COL_KR + LANES
COL_FG = COL_IG + M_WIDTH
D_IN_PAD = COL_FG + M_WIDTH

FF_CHUNKS = ((0, 768), (768, 1792), (1792, 2816))
KV_BLOCK = 256
KEY_SUB = 128

VMEM_LIMIT = 56 * 1024 * 1024

NT_DIMS = (((1,), (1,)), ((), ()))
TN_DIMS = (((0,), (0,)), ((), ()))


def _resident(block_shape, index):
    return pl.BlockSpec(block_shape, lambda *_: index, pipeline_mode=pl.Buffered(1))


def _rms(x, g):
    return x * lax.rsqrt(jnp.mean(x * x, axis=-1, keepdims=True) + RMS_EPS) * g


def _rope(x, ra, rbm, rbp):
    return x * ra + pltpu.roll(x, LANES - ROPE_HALF, axis=1) * rbm + pltpu.roll(x, ROPE_HALF, axis=1) * rbp


def _pre_mixer_kernel(x_ref, g_ref, win_ref, bg_ref, gq_ref, wuq_ref, gkv_ref, wuk_ref, wuvt_ref,
                      ra_ref, rbm_ref, rbp_ref,
                      qm_ref, km_ref, vm_ref, om_ref, ig_ref, lf_ref, qa_ref, ka_ref, vt_ref, ckv_ref, kr_ref):
    xb = _rms(x_ref[...], g_ref[...]).astype(BF16)

    def proj(col, width):
        return jnp.dot(xb, win_ref[:, col:col + width], preferred_element_type=F32)

    qm_ref[...] = proj(COL_QM, M_HEADS * M_DK).astype(BF16)
    km_ref[...] = (proj(COL_KM, M_HEADS * M_DK) * (M_DK ** -0.5)).astype(BF16)
    vm_ref[...] = proj(COL_VM, M_WIDTH).astype(BF16)
    om_ref[...] = proj(COL_OM, M_WIDTH)

    ig_ref[...] = proj(COL_IG, M_WIDTH) + bg_ref[0:1, :]
    fg = proj(COL_FG, M_WIDTH) + bg_ref[1:2, :]
    lf_ref[...] = jnp.minimum(fg, 0.0) - jnp.log1p(jnp.exp(-jnp.abs(fg)))

    ra, rbm, rbp = ra_ref[...], rbm_ref[...], rbp_ref[...]

    cqn = _rms(proj(COL_CQ, Q_RANK), gq_ref[...]).astype(BF16)
    q = jnp.dot(cqn, wuq_ref[...], preferred_element_type=F32)
    for h in range(A_HEADS):
        sl = slice(h * HEAD_SLOT, (h + 1) * HEAD_SLOT)
        qa_ref[:, sl] = (_rope(q[:, sl], ra, rbm, rbp) * (ATTN_SCALE * LOG2E)).astype(BF16)

    ckvn = _rms(proj(COL_CKV, KV_RANK), gkv_ref[...])
    ckv_ref[...] = ckvn
    ckvb = ckvn.astype(BF16)
    kr = _rope(proj(COL_KR, LANES), ra, rbm, rbp)
    kr_ref[...] = kr[:, ROPE_LANE0:ROPE_LANE0 + ROPE_DIM]
    kmat = jnp.dot(ckvb, wuk_ref[...], preferred_element_type=F32)
    for h in range(A_HEADS):
        sl = slice(h * HEAD_SLOT, (h + 1) * HEAD_SLOT)
        ka_ref[:, sl] = (kmat[:, sl] + kr).astype(BF16)
    vt = lax.dot_general(wuvt_ref[...], ckvb, NT_DIMS, preferred_element_type=F32)
    slot_row = lax.broadcasted_iota(jnp.int32, vt.shape, 0) % V_SLOT
    vt = jnp.where(slot_row == V_DIM, 1.0, vt).astype(BF16)
    for s in range(vt_ref.shape[0]):
        vt_ref[s] = vt[:, s * KV_BLOCK:(s + 1) * KV_BLOCK]


def _pre_mixer(x, layer, prm, rope_tabs, tm, n_rope_blocks, name):
    T = x.shape[0]
    ra, rbm, rbp = rope_tabs
    row = lambda w: pl.BlockSpec((tm, w), lambda i: (i, 0))
    rope_spec = pl.BlockSpec((tm, LANES), lambda i: (i % n_rope_blocks, 0))
    lyr = lambda shape: _resident((None,) + shape, (layer, 0, 0))
    out_widths = (M_HEADS * M_DK, M_HEADS * M_DK, M_WIDTH, M_WIDTH, M_WIDTH, M_WIDTH, QK_WIDTH, QK_WIDTH, None,
                  KV_RANK, ROPE_DIM)
    out_dtypes = (BF16, BF16, BF16, F32, F32, F32, BF16, BF16, BF16, F32, F32)
    kvb = tm // KV_BLOCK
    vt_spec = pl.BlockSpec((kvb, VT_ROWS, KV_BLOCK), lambda i: (i, 0, 0))
    vt_shape = jax.ShapeDtypeStruct((T // KV_BLOCK, VT_ROWS, KV_BLOCK), BF16)
    return pl.pallas_call(
        _pre_mixer_kernel,
        grid=(T // tm,),
        in_specs=[row(D_MODEL), lyr((1, D_MODEL)), lyr((D_MODEL, D_IN_PAD)), lyr((2, M_WIDTH)),
                  lyr((1, Q_RANK)), lyr((Q_RANK, QK_WIDTH)), lyr((1, KV_RANK)),
                  lyr((KV_RANK, QK_WIDTH)), lyr((VT_ROWS, KV_RANK)), rope_spec, rope_spec, rope_spec],
        out_specs=[vt_spec if w is None else row(w) for w in out_widths],
        out_shape=[vt_shape if w is None else jax.ShapeDtypeStruct((T, w), d)
                   for w, d in zip(out_widths, out_dtypes)],
        compiler_params=pltpu.CompilerParams(dimension_semantics=("arbitrary",), vmem_limit_bytes=VMEM_LIMIT),
        name=name,
    )(x, prm["g_pre_mix"], prm["w_in"], prm["b_gates"], prm["g_q_norm"], prm["w_uq"], prm["g_kv_norm"],
      prm["w_uk"], prm["w_uvt"], ra, rbm, rbp)


PAIRS = M_HEADS // 2
ST_COLS = LANES


def _split_bf16(x, parts):
    out = []
    for _ in range(parts - 1):
        piece = x.astype(BF16)
        out.append(piece)
        x = x - piece.astype(F32)
    out.append(x.astype(BF16))
    return out


def _dot_pieces(lhs, x, parts, lhs_first):
    acc = None
    for piece in _split_bf16(x, parts):
        d = (jnp.dot(lhs, piece, preferred_element_type=F32) if lhs_first
             else jnp.dot(piece, lhs, preferred_element_type=F32))
        acc = d if acc is None else acc + d
    return acc


def _cummax_rows(x, row):
    d = 1
    while d < x.shape[0]:
        x = jnp.where(row >= d, jnp.maximum(x, pltpu.roll(x, d, axis=0)), x)
        d *= 2
    return x


def _mlstm_kernel(q_ref, k_ref, v_ref, ig_ref, lf_ref, st0_ref, m0_ref, h_ref, st_ref, m_ref, *, L, n_sub):
    @pl.when(pl.program_id(1) == 0)
    def _():
        st_ref[...] = st0_ref[...]
        m_ref[...] = m0_ref[...]

    row_w = lax.broadcasted_iota(jnp.int32, (L, M_WIDTH), 0)
    row = lax.broadcasted_iota(jnp.int32, (L, LANES), 0)
    lane = lax.broadcasted_iota(jnp.int32, (L, LANES), 1)
    first_head = lane < M_DK
    key = lane % M_DK
    causal = key <= row
    diagonal = key == row
    tri = (lax.broadcasted_iota(jnp.int32, (L, L), 1) <= lax.broadcasted_iota(jnp.int32, (L, L), 0)).astype(BF16)
    st_row_first = lax.broadcasted_iota(jnp.int32, (LANES, LANES), 0) < M_DK
    ones_bd = (st_row_first == (lax.broadcasted_iota(jnp.int32, (LANES, LANES), 1) < M_DK)).astype(BF16)
    pairs = range(PAIRS)
    lanes_of = lambda p: slice(p * LANES, (p + 1) * LANES)

    def split_heads(x, fill_first=0, fill_second=0):
        return (jnp.where(first_head, x, jnp.full_like(x, fill_first)),
                jnp.where(first_head, jnp.full_like(x, fill_second), x))

    def stack_heads(x):
        parts = []
        for half in split_heads(x):
            parts.append(half)
            if L < M_DK:
                parts.append(jnp.zeros((M_DK - L, LANES), x.dtype))
        return jnp.concatenate(parts, axis=0)

    prefix = []
    for c in range(n_sub):
        rows = slice(c * L, (c + 1) * L)
        b = _dot_pieces(tri, lf_ref[rows, :], 2, True)
        a = ig_ref[rows, :] - b
        prefix.append((rows, b, a, _cummax_rows(a, row_w)))

    for rows, b, a, a_cummax in prefix:
        m0 = m_ref[...]
        mx = jnp.maximum(a_cummax, m0)
        mx_last = mx[L - 1:L, :]
        m_ref[...] = b[L - 1:L, :] + mx_last
        w_inter = jnp.exp(m0 - mx)
        clamp = jnp.exp(-(b + mx))
        w_key = jnp.exp(a - mx_last)
        decay = jnp.exp(m0 - mx_last)

        qs = [q_ref[rows, lanes_of(p)] for p in pairs]
        ks = [k_ref[rows, lanes_of(p)] for p in pairs]
        vs = [v_ref[rows, lanes_of(p)] for p in pairs]
        scores = [lax.dot_general(qs[p], stack_heads(ks[p]), NT_DIMS, preferred_element_type=F32) for p in pairs]
        inter = [jnp.dot(jnp.concatenate(split_heads(qs[p]), axis=0), st_ref[p].astype(BF16),
                         preferred_element_type=F32) for p in pairs]
        intra = []
        for p in pairs:
            a_row = jnp.sum(jnp.where(diagonal, a[:, lanes_of(p)], 0.0), axis=0, keepdims=True)
            d_m = jnp.where(causal, a_row, -jnp.inf)
            s_mat = (scores[p] * jnp.exp(d_m - mx[:, lanes_of(p)])).astype(BF16)
            intra.append(jnp.dot(s_mat, jnp.concatenate([stack_heads(vs[p]), ones_bd], axis=1),
                                 preferred_element_type=F32))
        cells = []
        for p in pairs:
            first, second = inter[p][:L], inter[p][L:]
            q_c = jnp.where(first_head, first, second)
            q_n = pltpu.roll(jnp.where(first_head, second, first), M_DK, axis=1)
            wi = w_inter[:, lanes_of(p)]
            num = wi * q_c + intra[p][:, :LANES]
            den = wi * q_n + intra[p][:, LANES:]
            h_cell = num / jnp.maximum(jnp.abs(den), clamp[:, lanes_of(p)])
            cells.append((h_cell, _dot_pieces(ones_bd, h_cell * h_cell, 2, False)))
        for p in pairs:
            h_cell, sum_sq = cells[p]
            h_ref[rows, lanes_of(p)] = h_cell * lax.rsqrt(sum_sq * (1.0 / M_DV) + RMS_EPS)
        for p in pairs:
            kw = (ks[p].astype(F32) * w_key[:, lanes_of(p)]).astype(BF16)
            upd = lax.dot_general(jnp.concatenate(split_heads(kw), axis=0),
                                  jnp.concatenate(split_heads(vs[p], 1, 1), axis=0), TN_DIMS,
                                  preferred_element_type=F32)
            dec = decay[:, lanes_of(p)]
            dec_rows = jnp.where(st_row_first, dec[:, 0:1], dec[:, M_DK:M_DK + 1])
            st_ref[p] = dec_rows * st_ref[p] + upd


def _mlstm(qm, km, vm, ig, lf, st0, m0, n_streams, n_chunks, L, n_sub, name):
    T = qm.shape[0]
    steps = n_chunks // n_sub
    tok = pl.BlockSpec((L * n_sub, M_WIDTH), lambda b, c: (b * steps + c, 0))
    st_spec = pl.BlockSpec((None, PAIRS, LANES, ST_COLS), lambda b, c: (b, 0, 0, 0))
    m_spec = pl.BlockSpec((None, 1, M_WIDTH), lambda b, c: (b, 0, 0))
    return pl.pallas_call(
        functools.partial(_mlstm_kernel, L=L, n_sub=n_sub),
        grid=(n_streams, steps),
        in_specs=[tok, tok, tok, tok, tok, st_spec, m_spec],
        out_specs=[tok, st_spec, m_spec],
        out_shape=[jax.ShapeDtypeStruct((T, M_WIDTH), F32),
                   jax.ShapeDtypeStruct((n_streams, PAIRS, LANES, ST_COLS), F32),
                   jax.ShapeDtypeStruct((n_streams, 1, M_WIDTH), F32)],
        compiler_params=pltpu.CompilerParams(dimension_semantics=("arbitrary", "arbitrary")),
        name=name,
    )(qm, km, vm, ig, lf, st0, m0)


def _pack_state(C, n):
    B = C.shape[0]
    c4 = C.astype(F32).reshape(B, PAIRS, 2, M_DK, M_DV)
    n4 = jnp.broadcast_to(n.astype(F32).reshape(B, PAIRS, 2, M_DK, 1), (B, PAIRS, 2, M_DK, M_DV))
    return jnp.concatenate([jnp.concatenate([c4[:, :, 0], n4[:, :, 0]], axis=-1),
                            jnp.concatenate([n4[:, :, 1], c4[:, :, 1]], axis=-1)], axis=-2)


def _unpack_state(st):
    B = st.shape[0]
    half = lambda hh: slice(hh * M_DK, (hh + 1) * M_DK)
    C = jnp.stack([st[:, :, half(hh), half(hh)] for hh in (0, 1)], axis=2).reshape(B, M_HEADS, M_DK, M_DV)
    n = jnp.stack([st[:, :, half(hh), (1 - hh) * M_DK] for hh in (0, 1)], axis=2).reshape(B, M_HEADS, M_DK)
    return C, n


def _attn_prompt_kernel(q_ref, k_ref, vt_ref, o_ref, acc_ref, m_ref):
    i = pl.program_id(1)
    tq = q_ref.shape[0]
    acc_ref[...] = jnp.zeros_like(acc_ref)
    m_ref[...] = jnp.full_like(m_ref, -jnp.inf)
    key = lax.broadcasted_iota(jnp.int32, (KEY_SUB, tq), 0)
    qry = lax.broadcasted_iota(jnp.int32, (KEY_SUB, tq), 1)
    items = [(h, sub) for h in range(A_HEADS) for sub in range(tq // KEY_SUB)]

    def block(j, masked):
        def scores(h, sub):
            hsl = slice(h * HEAD_SLOT, (h + 1) * HEAD_SLOT)
            rows = pl.ds(pl.multiple_of(j * tq + sub * KEY_SUB, KEY_SUB), KEY_SUB)
            return lax.dot_general(k_ref[rows, hsl], q_ref[:, hsl], NT_DIMS, preferred_element_type=F32)

        s_next = scores(*items[0])
        for n, (h, sub) in enumerate(items):
            hsl = slice(h * HEAD_SLOT, (h + 1) * HEAD_SLOT)
            s_t = s_next
            if n + 1 < len(items):
                s_next = scores(*items[n + 1])
            if masked:
                s_t = jnp.where((key + sub * KEY_SUB) // CHUNK <= qry // CHUNK, s_t, -jnp.inf)
            m_old = m_ref[h:h + 1, :]
            m_new = jnp.maximum(m_old, jnp.max(s_t, axis=0, keepdims=True))
            m_ref[h:h + 1, :] = m_new
            p_t = jnp.exp2(s_t - m_new).astype(BF16)
            vt = vt_ref[j, h * V_SLOT:(h + 1) * V_SLOT, sub * KEY_SUB:(sub + 1) * KEY_SUB]
            pv = jnp.dot(vt, p_t, preferred_element_type=F32)[:ACC_ROWS]
            acc_ref[h] = jnp.exp2(m_old - m_new) * acc_ref[h] + pv

    def body(j, carry):
        block(j, False)
        return carry

    lax.fori_loop(0, i, body, 0)
    block(i, True)
    outs = []
    for h in range(A_HEADS):
        acc = acc_ref[h]
        outs.append(acc[:V_DIM, :] / acc[V_DIM:V_DIM + 1, :])
    o_ref[...] = jnp.concatenate(outs, axis=0).T.astype(BF16)


def _attn_prompt(qa, ka, vt, n_streams, seq, name):
    T = qa.shape[0]
    tq = KV_BLOCK
    nq = seq // tq
    return pl.pallas_call(
        _attn_prompt_kernel,
        grid=(n_streams, nq),
        in_specs=[pl.BlockSpec((tq, QK_WIDTH), lambda b, i: (b * nq + i, 0)),
                  pl.BlockSpec((seq, QK_WIDTH), lambda b, i: (b, 0)),
                  pl.BlockSpec((nq, VT_ROWS, tq), lambda b, i: (b, 0, 0))],
        out_specs=pl.BlockSpec((tq, A_WIDTH), lambda b, i: (b * nq + i, 0)),
        out_shape=jax.ShapeDtypeStruct((T, A_WIDTH), BF16),
        scratch_shapes=[pltpu.VMEM((A_HEADS, ACC_ROWS, tq), F32), pltpu.VMEM((A_HEADS, tq), F32)],
        compiler_params=pltpu.CompilerParams(dimension_semantics=("arbitrary", "arbitrary"),
                                             vmem_limit_bytes=VMEM_LIMIT),
        name=name,
    )(qa, ka, vt)


def _attn_sample_kernel(q_ref, ckvn_ref, krn_ref, ckvp_ref, krp_ref, wukt_ref, wuvs_ref, place_ref, o_ref):
    L = q_ref.shape[0]
    q_heads = [q_ref[:, h * HEAD_SLOT:(h + 1) * HEAD_SLOT] for h in range(A_HEADS)]
    q_slot = jnp.concatenate(q_heads, axis=0)
    q_lat = jnp.concatenate([jnp.dot(q_heads[h], wukt_ref[h], preferred_element_type=F32)
                             for h in range(A_HEADS)], axis=0).astype(BF16)

    def scores(ckv, kr):
        ckvb = ckv.astype(BF16)
        kr_slot = jnp.dot(kr.astype(BF16), place_ref[...], preferred_element_type=F32).astype(BF16)
        s = (lax.dot_general(q_lat, ckvb, NT_DIMS, preferred_element_type=F32)
             + lax.dot_general(q_slot, kr_slot, NT_DIMS, preferred_element_type=F32))
        return s, ckvb

    s_p, ckvb_p = scores(ckvp_ref[...], krp_ref[...])
    s_n, ckvb_n = scores(ckvn_ref[...], krn_ref[...])
    m = jnp.maximum(jnp.max(s_p, axis=-1, keepdims=True), jnp.max(s_n, axis=-1, keepdims=True))
    p_p = jnp.exp2(s_p - m)
    p_n = jnp.exp2(s_n - m)
    denom = jnp.sum(p_p, axis=-1, keepdims=True) + jnp.sum(p_n, axis=-1, keepdims=True)
    o_lat = (jnp.dot(p_p.astype(BF16), ckvb_p, preferred_element_type=F32)
             + jnp.dot(p_n.astype(BF16), ckvb_n, preferred_element_type=F32)) / denom
    o_lat = o_lat.astype(BF16)
    for pair in range(A_HEADS // 2):
        h0, h1 = 2 * pair, 2 * pair + 1
        out = (jnp.dot(o_lat[h0 * L:(h0 + 1) * L], wuvs_ref[h0], preferred_element_type=F32)
               + jnp.dot(o_lat[h1 * L:(h1 + 1) * L], wuvs_ref[h1], preferred_element_type=F32))
        o_ref[:, pair * LANES:(pair + 1) * LANES] = out.astype(BF16)


def _attn_sample(qa, ckv_new, kr_new, cache_ckv, cache_krope, layer, prm, place, name):
    T = qa.shape[0]
    n_streams, past = cache_ckv.shape[1], cache_ckv.shape[2]
    L = T // n_streams
    tok = lambda w: pl.BlockSpec((L, w), lambda b: (b, 0))
    return pl.pallas_call(
        _attn_sample_kernel,
        grid=(n_streams,),
        in_specs=[tok(QK_WIDTH), tok(KV_RANK), tok(ROPE_DIM),
                  pl.BlockSpec((None, None, past, KV_RANK), lambda b: (layer, b, 0, 0)),
                  pl.BlockSpec((None, None, past, ROPE_DIM), lambda b: (layer, b, 0, 0)),
                  _resident((None, A_HEADS, HEAD_SLOT, KV_RANK), (layer, 0, 0, 0)),
                  _resident((None, A_HEADS, KV_RANK, LANES), (layer, 0, 0, 0)),
                  _resident((ROPE_DIM, HEAD_SLOT), (0, 0))],
        out_specs=tok(A_WIDTH),
        out_shape=jax.ShapeDtypeStruct((T, A_WIDTH), BF16),
        compiler_params=pltpu.CompilerParams(dimension_semantics=("arbitrary",), vmem_limit_bytes=VMEM_LIMIT),
        name=name,
    )(qa, ckv_new, kr_new, cache_ckv, cache_krope, prm["w_ukt"], prm["w_uvs"], place)


def _post_mixer_kernel(x_ref, hn_ref, om_ref, at_ref, gh_ref, wo_ref, gpm_ref, gpf_ref, w1_ref, w2_ref, gpo_ref,
                       o_ref):
    h_m = jax.nn.sigmoid(om_ref[...]) * (hn_ref[...] * gh_ref[...])
    mix = jnp.dot(jnp.concatenate([h_m.astype(BF16), at_ref[...]], axis=-1), wo_ref[...],
                  preferred_element_type=F32)
    x1 = x_ref[...] + _rms(mix, gpm_ref[...])
    hb = _rms(x1, gpf_ref[...]).astype(BF16)
    acc = None
    for lo, hi in FF_CHUNKS:
        gate = jnp.dot(hb, w1_ref[:, lo:hi], preferred_element_type=F32)
        up = jnp.dot(hb, w1_ref[:, D_FF + lo:D_FF + hi], preferred_element_type=F32)
        act = (gate * jax.nn.sigmoid(gate) * up).astype(BF16)
        part = jnp.dot(act, w2_ref[lo:hi, :], preferred_element_type=F32)
        acc = part if acc is None else acc + part
    o_ref[...] = x1 + _rms(acc, gpo_ref[...])


def _post_mixer(x, hn, om, attn, layer, prm, tm, name):
    T = x.shape[0]
    row = lambda w: pl.BlockSpec((tm, w), lambda i: (i, 0))
    lyr = lambda shape: _resident((None,) + shape, (layer, 0, 0))
    return pl.pallas_call(
        _post_mixer_kernel,
        grid=(T // tm,),
        in_specs=[row(D_MODEL), row(M_WIDTH), row(M_WIDTH), row(A_WIDTH), lyr((1, M_WIDTH)),
                  lyr((D_MODEL, D_MODEL)), lyr((1, D_MODEL)), lyr((1, D_MODEL)), lyr((D_MODEL, 2 * D_FF)),
                  lyr((D_FF, D_MODEL)), lyr((1, D_MODEL))],
        out_specs=row(D_MODEL),
        out_shape=jax.ShapeDtypeStruct((T, D_MODEL), F32),
        compiler_params=pltpu.CompilerParams(dimension_semantics=("arbitrary",), vmem_limit_bytes=VMEM_LIMIT),
        name=name,
    )(x, hn, om, attn, prm["g_mlstm_head"], prm["w_out"], prm["g_post_mix"], prm["g_pre_ffn"], prm["w_ffn_in"],
      prm["w_ffn_out"], prm["g_post_ffn"])


def _prepare_params(g_pre_mix, g_post_mix, g_pre_ffn, g_post_ffn, w_in, b_gates, g_mlstm_head, g_q_norm, w_uq,
                    g_kv_norm, w_ukv, w_out, w_ffn_in, w_ffn_out):
    depth = w_in.shape[0]
    sizes = (M_HEADS * M_DK, M_HEADS * M_DK, M_WIDTH, M_WIDTH, 2 * M_HEADS, Q_RANK, KV_RANK, ROPE_DIM)
    offs = [0]
    for s in sizes:
        offs.append(offs[-1] + s)
    part = lambda i: w_in[:, :, offs[i]:offs[i + 1]]
    zeros = lambda w: jnp.zeros((depth, D_MODEL, w), w_in.dtype)
    per_head = lambda g: jnp.repeat(g, M_DV, axis=-1)
    w_in_r = jnp.concatenate(
        [part(0), part(1), part(2), part(3), part(5), part(6),
         zeros(ROPE_LANE0), part(7), zeros(LANES - ROPE_LANE0 - ROPE_DIM),
         per_head(part(4)[..., :M_HEADS]), per_head(part(4)[..., M_HEADS:])], axis=-1).astype(BF16)
    b_gates = b_gates.astype(F32)
    b_gates_e = jnp.stack([per_head(b_gates[:, :M_HEADS]), per_head(b_gates[:, M_HEADS:])], axis=1)
    w_uq_r = jnp.pad(w_uq.reshape(depth, Q_RANK, A_HEADS, NOPE_DIM + ROPE_DIM),
                     ((0, 0), (0, 0), (0, 0), (0, HEAD_SLOT - NOPE_DIM - ROPE_DIM)))
    w_uq_r = w_uq_r.reshape(depth, Q_RANK, QK_WIDTH).astype(BF16)
    w_ukv4 = w_ukv.reshape(depth, KV_RANK, A_HEADS, NOPE_DIM + V_DIM)
    w_uk4 = jnp.pad(w_ukv4[..., :NOPE_DIM], ((0, 0), (0, 0), (0, 0), (0, HEAD_SLOT - NOPE_DIM)))
    w_uk_r = w_uk4.reshape(depth, KV_RANK, QK_WIDTH).astype(BF16)
    w_ukt = w_uk4.transpose(0, 2, 3, 1).astype(BF16)
    w_uv4 = w_ukv4[..., NOPE_DIM:]
    w_uvt = jnp.pad(w_uv4, ((0, 0), (0, 0), (0, 0), (0, V_SLOT - V_DIM)))
    w_uvt = w_uvt.reshape(depth, KV_RANK, VT_ROWS).transpose(0, 2, 1).astype(BF16)
    w_uvs = jnp.stack([jnp.pad(w_uv4[:, :, h, :], ((0, 0), (0, 0), ((h % 2) * V_DIM, (1 - h % 2) * V_DIM)))
                       for h in range(A_HEADS)], axis=1).astype(BF16)
    row = lambda g: g.astype(F32)[:, None, :]
    return {
        "g_pre_mix": row(g_pre_mix), "g_post_mix": row(g_post_mix), "g_pre_ffn": row(g_pre_ffn),
        "g_post_ffn": row(g_post_ffn), "g_mlstm_head": row(g_mlstm_head), "g_q_norm": row(g_q_norm),
        "g_kv_norm": row(g_kv_norm),
        "b_gates": b_gates_e,
        "w_in": w_in_r, "w_uq": w_uq_r, "w_uk": w_uk_r, "w_ukt": w_ukt, "w_uvt": w_uvt, "w_uvs": w_uvs,
        "w_out": w_out.astype(BF16), "w_ffn_in": w_ffn_in.astype(BF16), "w_ffn_out": w_ffn_out.astype(BF16),
    }


def _rope_tables(pos):
    inv = ROPE_THETA ** (-jnp.arange(ROPE_HALF, dtype=F32) / ROPE_HALF)
    ang = pos.astype(F32)[:, None] * inv[None, :]
    cos, sin = jnp.cos(ang), jnp.sin(ang)
    n = pos.shape[0]
    z = lambda w: jnp.zeros((n, w), F32)
    tail = LANES - ROPE_LANE0 - ROPE_DIM
    ra = jnp.concatenate([jnp.ones((n, ROPE_LANE0), F32), cos, cos, z(tail)], axis=-1)
    rbm = jnp.concatenate([z(ROPE_LANE0), -sin, z(ROPE_HALF), z(tail)], axis=-1)
    rbp = jnp.concatenate([z(ROPE_LANE0), z(ROPE_HALF), sin, z(tail)], axis=-1)
    return ra, rbm, rbp


def _krope_placement():
    r = lax.broadcasted_iota(jnp.int32, (ROPE_DIM, HEAD_SLOT), 0)
    c = lax.broadcasted_iota(jnp.int32, (ROPE_DIM, HEAD_SLOT), 1)
    return (c == (r + ROPE_LANE0)).astype(BF16)


TM_PROMPT = 512
TM_SAMPLE = 256
MLSTM_SUB = 4


def kernel(x_prompt, x_sample, cache_ckv, cache_krope, state_mlstm_C, state_mlstm_n, state_mlstm_m,
           g_pre_mix, g_post_mix, g_pre_ffn, g_post_ffn, w_in, b_gates, g_mlstm_head,
           g_q_norm, w_uq, g_kv_norm, w_ukv, w_out, w_ffn_in, w_ffn_out):
    B, S, _ = x_prompt.shape
    Bs, Ls, _ = x_sample.shape
    depth, _, past, _ = cache_ckv.shape
    assert S % TM_PROMPT == 0 and TM_PROMPT % KV_BLOCK == 0 and KV_BLOCK % CHUNK == 0
    assert TM_SAMPLE % Ls == 0 and TM_SAMPLE % KV_BLOCK == 0 and (Bs * Ls) % TM_SAMPLE == 0

    prm = _prepare_params(g_pre_mix, g_post_mix, g_pre_ffn, g_post_ffn, w_in, b_gates, g_mlstm_head, g_q_norm,
                          w_uq, g_kv_norm, w_ukv, w_out, w_ffn_in, w_ffn_out)
    rope_p = _rope_tables(jnp.arange(S))
    rope_s = _rope_tables(past + jnp.arange(TM_SAMPLE) % Ls)
    place = _krope_placement()

    xp = x_prompt.reshape(B * S, D_MODEL)
    xs = x_sample.reshape(Bs * Ls, D_MODEL)
    zero_st = jnp.zeros((B, PAIRS, LANES, ST_COLS), F32)
    zero_m = jnp.zeros((B, 1, M_WIDTH), F32)
    m_in = jnp.repeat(state_mlstm_m.astype(F32), M_DV, axis=-1)[:, :, None, :]

    outs_p = {k: [] for k in ("ckv", "kr", "C", "n", "m")}
    outs_s = {k: [] for k in ("ckv", "kr", "C", "n", "m")}
    for l in range(depth):
        qm, km, vm, om, ig, lf, qa, ka, vt, ckv, kr = _pre_mixer(
            xp, l, prm, rope_p, TM_PROMPT, S // TM_PROMPT, f"pre_mixer_p{l}")
        hn, st, m = _mlstm(qm, km, vm, ig, lf, zero_st, zero_m, B, S // CHUNK, CHUNK, MLSTM_SUB, f"mlstm_p{l}")
        attn = _attn_prompt(qa, ka, vt, B, S, f"attn_p{l}")
        xp = _post_mixer(xp, hn, om, attn, l, prm, TM_PROMPT, f"post_mixer_p{l}")
        for key, val in zip(("ckv", "kr", "C", "n", "m"), (ckv, kr) + _unpack_state(st) + (m,)):
            outs_p[key].append(val)
        qm, km, vm, om, ig, lf, qa, _, _, ckv, kr = _pre_mixer(
            xs, l, prm, rope_s, TM_SAMPLE, 1, f"pre_mixer_s{l}")
        hn, st, m = _mlstm(qm, km, vm, ig, lf, _pack_state(state_mlstm_C[l], state_mlstm_n[l]), m_in[l],
                           Bs, 1, Ls, 1, f"mlstm_s{l}")
        attn = _attn_sample(qa, ckv, kr, cache_ckv, cache_krope, l, prm, place, f"attn_s{l}")
        xs = _post_mixer(xs, hn, om, attn, l, prm, TM_SAMPLE, f"post_mixer_s{l}")
        for key, val in zip(("ckv", "kr", "C", "n", "m"), (ckv, kr) + _unpack_state(st) + (m,)):
            outs_s[key].append(val)

    def collect(o, nb, ln):
        return (jnp.stack(o["ckv"]).reshape(depth, nb, ln, KV_RANK),
                jnp.stack(o["kr"]).reshape(depth, nb, ln, ROPE_DIM),
                jnp.stack(o["C"]), jnp.stack(o["n"]), jnp.stack(o["m"])[:, :, 0, ::M_DV])

    return ((xp.reshape(B, S, D_MODEL), xs.reshape(Bs, Ls, D_MODEL))
            + collect(outs_p, B, S) + collect(outs_s, Bs, Ls))
```

```python
import functools

import jax
import jax.numpy as jnp
from jax import lax
from jax.experimental import pallas as pl
from jax.experimental.pallas import tpu as pltpu

F32 = jnp.float32
BF16 = jnp.bfloat16

D_MODEL = 1024
DEPTH = 4
CHUNK = 64
M_HEADS = 8
M_DK = 64
M_DV = 64
M_WIDTH = M_HEADS * M_DV
A_HEADS = 8
NOPE_DIM = 64
ROPE_DIM = 32
ROPE_HALF = ROPE_DIM // 2
V_DIM = 64
A_WIDTH = A_HEADS * V_DIM
Q_RANK = 256
KV_RANK = 256
ROPE_THETA = 10000.0
D_FF = 2816
RMS_EPS = 1e-6
ATTN_SCALE = (NOPE_DIM + ROPE_DIM) ** -0.5

LANES = 128
HEAD_SLOT = LANES
QK_WIDTH = A_HEADS * HEAD_SLOT
V_SLOT = LANES
VT_ROWS = A_HEADS * V_SLOT
ACC_ROWS = V_DIM + 16
LOG2E = 1.4426950408889634
ROPE_LANE0 = NOPE_DIM

COL_QM = 0
COL_KM = COL_QM + M_HEADS * M_DK
COL_VM = COL_KM + M_HEADS * M_DK
COL_OM = COL_VM + M_WIDTH
COL_CQ = COL_OM + M_WIDTH
COL_CKV = COL_CQ + Q_RANK
COL_KR = COL_CKV + KV_RANK
COL_G = COL_KR + LANES
D_IN_PAD = COL_G + LANES

FF_CHUNKS = ((0, 768), (768, 1792), (1792, 2816))
KV_BLOCK = 256
KEY_SUB = 128

VMEM_LIMIT = 56 * 1024 * 1024

NT_DIMS = (((1,), (1,)), ((), ()))
TN_DIMS = (((0,), (0,)), ((), ()))


def _resident(block_shape, index):
    return pl.BlockSpec(block_shape, lambda *_: index, pipeline_mode=pl.Buffered(1))


def _rms(x, g):
    return x * lax.rsqrt(jnp.mean(x * x, axis=-1, keepdims=True) + RMS_EPS) * g


def _rope(x, ra, rbm, rbp):
    return x * ra + pltpu.roll(x, LANES - ROPE_HALF, axis=1) * rbm + pltpu.roll(x, ROPE_HALF, axis=1) * rbp


def _pre_mixer_kernel(x_ref, g_ref, win_ref, bg_ref, gq_ref, wuq_ref, gkv_ref, wuk_ref, wuvt_ref,
                      ra_ref, rbm_ref, rbp_ref,
                      qm_ref, km_ref, vm_ref, om_ref, gate_ref, qa_ref, ka_ref, vt_ref, ckv_ref, kr_ref):
    xb = _rms(x_ref[...], g_ref[...]).astype(BF16)

    def proj(col, width):
        return jnp.dot(xb, win_ref[:, col:col + width], preferred_element_type=F32)

    qm_ref[...] = proj(COL_QM, M_HEADS * M_DK).astype(BF16)
    km_ref[...] = (proj(COL_KM, M_HEADS * M_DK) * (M_DK ** -0.5)).astype(BF16)
    vm_ref[...] = proj(COL_VM, M_WIDTH).astype(BF16)
    om_ref[...] = proj(COL_OM, M_WIDTH)

    gates = proj(COL_G, LANES) + bg_ref[...]
    lane = lax.broadcasted_iota(jnp.int32, gates.shape, 1)
    log_f = jnp.minimum(gates, 0.0) - jnp.log1p(jnp.exp(-jnp.abs(gates)))
    gate_ref[...] = jnp.where(lane < M_HEADS, gates, log_f)

    ra, rbm, rbp = ra_ref[...], rbm_ref[...], rbp_ref[...]

    cqn = _rms(proj(COL_CQ, Q_RANK), gq_ref[...]).astype(BF16)
    q = jnp.dot(cqn, wuq_ref[...], preferred_element_type=F32)
    for h in range(A_HEADS):
        sl = slice(h * HEAD_SLOT, (h + 1) * HEAD_SLOT)
        qa_ref[:, sl] = (_rope(q[:, sl], ra, rbm, rbp) * (ATTN_SCALE * LOG2E)).astype(BF16)

    ckvn = _rms(proj(COL_CKV, KV_RANK), gkv_ref[...])
    ckv_ref[...] = ckvn
    ckvb = ckvn.astype(BF16)
    kr = _rope(proj(COL_KR, LANES), ra, rbm, rbp)
    kr_ref[...] = kr[:, ROPE_LANE0:ROPE_LANE0 + ROPE_DIM]
    kmat = jnp.dot(ckvb, wuk_ref[...], preferred_element_type=F32)
    for h in range(A_HEADS):
        sl = slice(h * HEAD_SLOT, (h + 1) * HEAD_SLOT)
        ka_ref[:, sl] = (kmat[:, sl] + kr).astype(BF16)
    vt = lax.dot_general(wuvt_ref[...], ckvb, NT_DIMS, preferred_element_type=F32)
    slot_row = lax.broadcasted_iota(jnp.int32, vt.shape, 0) % V_SLOT
    vt = jnp.where(slot_row == V_DIM, 1.0, vt).astype(BF16)
    for s in range(vt_ref.shape[0]):
        vt_ref[s] = vt[:, s * KV_BLOCK:(s + 1) * KV_BLOCK]


def _pre_mixer(x, layer, prm, rope_tabs, tm, n_rope_blocks, name):
    T = x.shape[0]
    ra, rbm, rbp = rope_tabs
    row = lambda w: pl.BlockSpec((tm, w), lambda i: (i, 0))
    rope_spec = pl.BlockSpec((tm, LANES), lambda i: (i % n_rope_blocks, 0))
    lyr = lambda shape: _resident((None,) + shape, (layer, 0, 0))
    out_widths = (M_HEADS * M_DK, M_HEADS * M_DK, M_WIDTH, M_WIDTH, LANES, QK_WIDTH, QK_WIDTH, None,
                  KV_RANK, ROPE_DIM)
    out_dtypes = (BF16, BF16, BF16, F32, F32, BF16, BF16, BF16, F32, F32)
    kvb = tm // KV_BLOCK
    vt_spec = pl.BlockSpec((kvb, VT_ROWS, KV_BLOCK), lambda i: (i, 0, 0))
    vt_shape = jax.ShapeDtypeStruct((T // KV_BLOCK, VT_ROWS, KV_BLOCK), BF16)
    return pl.pallas_call(
        _pre_mixer_kernel,
        grid=(T // tm,),
        in_specs=[row(D_MODEL), lyr((1, D_MODEL)), lyr((D_MODEL, D_IN_PAD)), lyr((1, LANES)),
                  lyr((1, Q_RANK)), lyr((Q_RANK, QK_WIDTH)), lyr((1, KV_RANK)),
                  lyr((KV_RANK, QK_WIDTH)), lyr((VT_ROWS, KV_RANK)), rope_spec, rope_spec, rope_spec],
        out_specs=[vt_spec if w is None else row(w) for w in out_widths],
        out_shape=[vt_shape if w is None else jax.ShapeDtypeStruct((T, w), d)
                   for w, d in zip(out_widths, out_dtypes)],
        compiler_params=pltpu.CompilerParams(dimension_semantics=("arbitrary",), vmem_limit_bytes=VMEM_LIMIT),
        name=name,
    )(x, prm["g_pre_mix"], prm["w_in"], prm["b_gates"], prm["g_q_norm"], prm["w_uq"], prm["g_kv_norm"],
      prm["w_uk"], prm["w_uvt"], ra, rbm, rbp)


PAIRS = M_HEADS // 2
ST_COLS = LANES


def _split_bf16(x, parts):
    out = []
    for _ in range(parts - 1):
        piece = x.astype(BF16)
        out.append(piece)
        x = x - piece.astype(F32)
    out.append(x.astype(BF16))
    return out


def _dot_pieces(lhs, x, parts, lhs_first):
    acc = None
    for piece in _split_bf16(x, parts):
        d = (jnp.dot(lhs, piece, preferred_element_type=F32) if lhs_first
             else jnp.dot(piece, lhs, preferred_element_type=F32))
        acc = d if acc is None else acc + d
    return acc


def _cummax_rows(x, row):
    d = 1
    while d < x.shape[0]:
        x = jnp.where(row >= d, jnp.maximum(x, pltpu.roll(x, d, axis=0)), x)
        d *= 2
    return x


def _mlstm_kernel(q_ref, k_ref, v_ref, g_ref, st0_ref, m0_ref, h_ref, st_ref, m_ref, *, L, n_sub):
    @pl.when(pl.program_id(1) == 0)
    def _():
        st_ref[...] = st0_ref[...]
        m_ref[...] = m0_ref[...]

    row_w = lax.broadcasted_iota(jnp.int32, (L, M_WIDTH), 0)
    row = lax.broadcasted_iota(jnp.int32, (L, LANES), 0)
    lane = lax.broadcasted_iota(jnp.int32, (L, LANES), 1)
    first_head = lane < M_DK
    key = lane % M_DK
    causal = key <= row
    diagonal = key == row
    tri = (lax.broadcasted_iota(jnp.int32, (L, L), 1) <= lax.broadcasted_iota(jnp.int32, (L, L), 0)).astype(BF16)
    expand = (lax.broadcasted_iota(jnp.int32, (LANES, 2 * M_WIDTH), 0)
              == lax.broadcasted_iota(jnp.int32, (LANES, 2 * M_WIDTH), 1) // M_DV).astype(BF16)
    gate_lane = lax.broadcasted_iota(jnp.int32, (L, LANES), 1)
    st_row_first = lax.broadcasted_iota(jnp.int32, (LANES, LANES), 0) < M_DK
    ones_bd = (st_row_first == (lax.broadcasted_iota(jnp.int32, (LANES, LANES), 1) < M_DK)).astype(BF16)
    pairs = range(PAIRS)
    lanes_of = lambda p: slice(p * LANES, (p + 1) * LANES)

    def split_heads(x, fill_first=0, fill_second=0):
        return (jnp.where(first_head, x, jnp.full_like(x, fill_first)),
                jnp.where(first_head, jnp.full_like(x, fill_second), x))

    def stack_heads(x):
        parts = []
        for half in split_heads(x):
            parts.append(half)
            if L < M_DK:
                parts.append(jnp.zeros((M_DK - L, LANES), x.dtype))
        return jnp.concatenate(parts, axis=0)

    def stage_gates(c):
        rows = slice(c * L, (c + 1) * L)
        gates = g_ref[rows, :]
        csum = _dot_pieces(tri, gates, 3, True)
        wide = _dot_pieces(expand, jnp.where(gate_lane < M_HEADS, gates, csum), 2, False)
        b = wide[:, M_WIDTH:]
        a = wide[:, :M_WIDTH] - b
        return dict(rows=rows, b=b, a=a, a_cummax=_cummax_rows(a, row_w))

    def stage_issue(ck):
        rows, b, a = ck["rows"], ck["b"], ck["a"]
        m0 = m_ref[...]
        mx = jnp.maximum(ck["a_cummax"], m0)
        mx_last = mx[L - 1:L, :]
        m_ref[...] = b[L - 1:L, :] + mx_last
        ck.update(mx=mx, w_inter=jnp.exp(m0 - mx), clamp=jnp.exp(-(b + mx)), decay=jnp.exp(m0 - mx_last))
        w_key = jnp.exp(a - mx_last)
        qs = [q_ref[rows, lanes_of(p)] for p in pairs]
        ks = [k_ref[rows, lanes_of(p)] for p in pairs]
        ck["vs"] = vs = [v_ref[rows, lanes_of(p)] for p in pairs]
        ck["scores"] = [lax.dot_general(qs[p], stack_heads(ks[p]), NT_DIMS, preferred_element_type=F32)
                        for p in pairs]
        ck["inter"] = [jnp.dot(jnp.concatenate(split_heads(qs[p]), axis=0), st_ref[p].astype(BF16),
                               preferred_element_type=F32) for p in pairs]
        upd = []
        for p in pairs:
            kw = (ks[p].astype(F32) * w_key[:, lanes_of(p)]).astype(BF16)
            upd.append(lax.dot_general(jnp.concatenate(split_heads(kw), axis=0),
                                       jnp.concatenate(split_heads(vs[p], 1, 1), axis=0), TN_DIMS,
                                       preferred_element_type=F32))
        ck["upd"] = upd

    def stage_state(ck):
        for p in pairs:
            dec = ck["decay"][:, lanes_of(p)]
            dec_rows = jnp.where(st_row_first, dec[:, 0:1], dec[:, M_DK:M_DK + 1])
            st_ref[p] = dec_rows * st_ref[p] + ck["upd"][p]

    def stage_intra(ck):
        intra = []
        for p in pairs:
            a_row = jnp.sum(jnp.where(diagonal, ck["a"][:, lanes_of(p)], 0.0), axis=0, keepdims=True)
            d_m = jnp.where(causal, a_row, -jnp.inf)
            s_mat = (ck["scores"][p] * jnp.exp(d_m - ck["mx"][:, lanes_of(p)])).astype(BF16)
            intra.append(jnp.dot(s_mat, jnp.concatenate([stack_heads(ck["vs"][p]), ones_bd], axis=1),
                                 preferred_element_type=F32))
        ck["intra"] = intra

    def stage_cell(ck):
        cells = []
        for p in pairs:
            first, second = ck["inter"][p][:L], ck["inter"][p][L:]
            q_c = jnp.where(first_head, first, second)
            q_n = pltpu.roll(jnp.where(first_head, second, first), M_DK, axis=1)
            wi = ck["w_inter"][:, lanes_of(p)]
            num = wi * q_c + ck["intra"][p][:, :LANES]
            den = wi * q_n + ck["intra"][p][:, LANES:]
            h_cell = num / jnp.maximum(jnp.abs(den), ck["clamp"][:, lanes_of(p)])
            cells.append((h_cell, _dot_pieces(ones_bd, h_cell * h_cell, 2, False)))
        ck["cells"] = cells

    def stage_out(ck):
        for p in pairs:
            h_cell, sum_sq = ck["cells"][p]
            h_ref[ck["rows"], lanes_of(p)] = h_cell * lax.rsqrt(sum_sq * (1.0 / M_DV) + RMS_EPS)

    chunks = [stage_gates(c) for c in range(n_sub)]
    for t in range(n_sub + 3):
        live = lambda c: 0 <= c < n_sub
        if live(t - 1):
            stage_state(chunks[t - 1])
        if live(t):
            stage_issue(chunks[t])
        if live(t - 1):
            stage_intra(chunks[t - 1])
        if live(t - 2):
            stage_cell(chunks[t - 2])
        if live(t - 3):
            stage_out(chunks[t - 3])


def _mlstm(qm, km, vm, gates, st0, m0, n_streams, n_chunks, L, n_sub, name):
    T = qm.shape[0]
    steps = n_chunks // n_sub
    tok = pl.BlockSpec((L * n_sub, M_WIDTH), lambda b, c: (b * steps + c, 0))
    gate_spec = pl.BlockSpec((L * n_sub, LANES), lambda b, c: (b * steps + c, 0))
    st_spec = pl.BlockSpec((None, PAIRS, LANES, ST_COLS), lambda b, c: (b, 0, 0, 0))
    m_spec = pl.BlockSpec((None, 1, M_WIDTH), lambda b, c: (b, 0, 0))
    return pl.pallas_call(
        functools.partial(_mlstm_kernel, L=L, n_sub=n_sub),
        grid=(n_streams, steps),
        in_specs=[tok, tok, tok, gate_spec, st_spec, m_spec],
        out_specs=[tok, st_spec, m_spec],
        out_shape=[jax.ShapeDtypeStruct((T, M_WIDTH), F32),
                   jax.ShapeDtypeStruct((n_streams, PAIRS, LANES, ST_COLS), F32),
                   jax.ShapeDtypeStruct((n_streams, 1, M_WIDTH), F32)],
        compiler_params=pltpu.CompilerParams(dimension_semantics=("arbitrary", "arbitrary")),
        name=name,
    )(qm, km, vm, gates, st0, m0)


def _pack_state(C, n):
    B = C.shape[0]
    c4 = C.astype(F32).reshape(B, PAIRS, 2, M_DK, M_DV)
    n4 = jnp.broadcast_to(n.astype(F32).reshape(B, PAIRS, 2, M_DK, 1), (B, PAIRS, 2, M_DK, M_DV))
    return jnp.concatenate([jnp.concatenate([c4[:, :, 0], n4[:, :, 0]], axis=-1),
                            jnp.concatenate([n4[:, :, 1], c4[:, :, 1]], axis=-1)], axis=-2)


def _unpack_state(st):
    B = st.shape[0]
    half = lambda hh: slice(hh * M_DK, (hh + 1) * M_DK)
    C = jnp.stack([st[:, :, half(hh), half(hh)] for hh in (0, 1)], axis=2).reshape(B, M_HEADS, M_DK, M_DV)
    n = jnp.stack([st[:, :, half(hh), (1 - hh) * M_DK] for hh in (0, 1)], axis=2).reshape(B, M_HEADS, M_DK)
    return C, n


def _attn_prompt_kernel(q_ref, k_ref, vt_ref, o_ref, acc_ref, m_ref):
    i = pl.program_id(1)
    tq = q_ref.shape[0]
    acc_ref[...] = jnp.zeros_like(acc_ref)
    m_ref[...] = jnp.full_like(m_ref, -jnp.inf)
    key = lax.broadcasted_iota(jnp.int32, (KEY_SUB, tq), 0)
    qry = lax.broadcasted_iota(jnp.int32, (KEY_SUB, tq), 1)
    items = [(h, sub) for h in range(A_HEADS) for sub in range(tq // KEY_SUB)]

    def block(j, masked):
        def scores(h, sub):
            hsl = slice(h * HEAD_SLOT, (h + 1) * HEAD_SLOT)
            rows = pl.ds(pl.multiple_of(j * tq + sub * KEY_SUB, KEY_SUB), KEY_SUB)
            return lax.dot_general(k_ref[rows, hsl], q_ref[:, hsl], NT_DIMS, preferred_element_type=F32)

        s_next = scores(*items[0])
        for n, (h, sub) in enumerate(items):
            s_t = s_next
            if n + 1 < len(items):
                s_next = scores(*items[n + 1])
            if masked:
                s_t = jnp.where((key + sub * KEY_SUB) // CHUNK <= qry // CHUNK, s_t, -jnp.inf)
            m_old = m_ref[h:h + 1, :]
            m_new = jnp.maximum(m_old, jnp.max(s_t, axis=0, keepdims=True))
            m_ref[h:h + 1, :] = m_new
            p_t = jnp.exp2(s_t - m_new).astype(BF16)
            vt = vt_ref[j, h * V_SLOT:(h + 1) * V_SLOT, sub * KEY_SUB:(sub + 1) * KEY_SUB]
            pv = jnp.dot(vt, p_t, preferred_element_type=F32)[:ACC_ROWS]
            acc_ref[h] = jnp.exp2(m_old - m_new) * acc_ref[h] + pv

    def body(j, carry):
        block(j, False)
        return carry

    lax.fori_loop(0, i, body, 0)
    block(i, True)
    outs = []
    for h in range(A_HEADS):
        acc = acc_ref[h]
        outs.append(acc[:V_DIM, :] / acc[V_DIM:V_DIM + 1, :])
    o_ref[...] = jnp.concatenate(outs, axis=0).T.astype(BF16)


def _attn_prompt(qa, ka, vt, n_streams, seq, name):
    T = qa.shape[0]
    tq = KV_BLOCK
    nq = seq // tq
    return pl.pallas_call(
        _attn_prompt_kernel,
        grid=(n_streams, nq),
        in_specs=[pl.BlockSpec((tq, QK_WIDTH), lambda b, i: (b * nq + i, 0)),
                  pl.BlockSpec((seq, QK_WIDTH), lambda b, i: (b, 0)),
                  pl.BlockSpec((nq, VT_ROWS, tq), lambda b, i: (b, 0, 0))],
        out_specs=pl.BlockSpec((tq, A_WIDTH), lambda b, i: (b * nq + i, 0)),
        out_shape=jax.ShapeDtypeStruct((T, A_WIDTH), BF16),
        scratch_shapes=[pltpu.VMEM((A_HEADS, ACC_ROWS, tq), F32), pltpu.VMEM((A_HEADS, tq), F32)],
        compiler_params=pltpu.CompilerParams(dimension_semantics=("arbitrary", "arbitrary"),
                                             vmem_limit_bytes=VMEM_LIMIT),
        name=name,
    )(qa, ka, vt)


def _attn_sample_kernel(q_ref, ckvn_ref, krn_ref, ckvp_ref, krp_ref, wukt_ref, wuvs_ref, place_ref, o_ref):
    L = q_ref.shape[0]
    q_heads = [q_ref[:, h * HEAD_SLOT:(h + 1) * HEAD_SLOT] for h in range(A_HEADS)]
    q_slot = jnp.concatenate(q_heads, axis=0)
    q_lat = jnp.concatenate([jnp.dot(q_heads[h], wukt_ref[h], preferred_element_type=F32)
                             for h in range(A_HEADS)], axis=0).astype(BF16)

    def scores(ckv, kr):
        ckvb = ckv.astype(BF16)
        kr_slot = jnp.dot(kr.astype(BF16), place_ref[...], preferred_element_type=F32).astype(BF16)
        s = (lax.dot_general(q_lat, ckvb, NT_DIMS, preferred_element_type=F32)
             + lax.dot_general(q_slot, kr_slot, NT_DIMS, preferred_element_type=F32))
        return s, ckvb

    s_p, ckvb_p = scores(ckvp_ref[...], krp_ref[...])
    s_n, ckvb_n = scores(ckvn_ref[...], krn_ref[...])
    m = jnp.maximum(jnp.max(s_p, axis=-1, keepdims=True), jnp.max(s_n, axis=-1, keepdims=True))
    p_p = jnp.exp2(s_p - m)
    p_n = jnp.exp2(s_n - m)
    denom = jnp.sum(p_p, axis=-1, keepdims=True) + jnp.sum(p_n, axis=-1, keepdims=True)
    o_lat = (jnp.dot(p_p.astype(BF16), ckvb_p, preferred_element_type=F32)
             + jnp.dot(p_n.astype(BF16), ckvb_n, preferred_element_type=F32)) / denom
    o_lat = o_lat.astype(BF16)
    for pair in range(A_HEADS // 2):
        h0, h1 = 2 * pair, 2 * pair + 1
        out = (jnp.dot(o_lat[h0 * L:(h0 + 1) * L], wuvs_ref[h0], preferred_element_type=F32)
               + jnp.dot(o_lat[h1 * L:(h1 + 1) * L], wuvs_ref[h1], preferred_element_type=F32))
        o_ref[:, pair * LANES:(pair + 1) * LANES] = out.astype(BF16)


def _attn_sample(qa, ckv_new, kr_new, cache_ckv, cache_krope, layer, prm, place, name):
    T = qa.shape[0]
    n_streams, past = cache_ckv.shape[1], cache_ckv.shape[2]
    L = T // n_streams
    tok = lambda w: pl.BlockSpec((L, w), lambda b: (b, 0))
    return pl.pallas_call(
        _attn_sample_kernel,
        grid=(n_streams,),
        in_specs=[tok(QK_WIDTH), tok(KV_RANK), tok(ROPE_DIM),
                  pl.BlockSpec((None, None, past, KV_RANK), lambda b: (layer, b, 0, 0)),
                  pl.BlockSpec((None, None, past, ROPE_DIM), lambda b: (layer, b, 0, 0)),
                  _resident((None, A_HEADS, HEAD_SLOT, KV_RANK), (layer, 0, 0, 0)),
                  _resident((None, A_HEADS, KV_RANK, LANES), (layer, 0, 0, 0)),
                  _resident((ROPE_DIM, HEAD_SLOT), (0, 0))],
        out_specs=tok(A_WIDTH),
        out_shape=jax.ShapeDtypeStruct((T, A_WIDTH), BF16),
        compiler_params=pltpu.CompilerParams(dimension_semantics=("arbitrary",), vmem_limit_bytes=VMEM_LIMIT),
        name=name,
    )(qa, ckv_new, kr_new, cache_ckv, cache_krope, prm["w_ukt"], prm["w_uvs"], place)


def _post_mixer_kernel(x_ref, hn_ref, om_ref, at_ref, gh_ref, wo_ref, gpm_ref, gpf_ref, w1_ref, w2_ref, gpo_ref,
                       o_ref):
    h_m = jax.nn.sigmoid(om_ref[...]) * (hn_ref[...] * gh_ref[...])
    mix = jnp.dot(jnp.concatenate([h_m.astype(BF16), at_ref[...]], axis=-1), wo_ref[...],
                  preferred_element_type=F32)
    x1 = x_ref[...] + _rms(mix, gpm_ref[...])
    hb = _rms(x1, gpf_ref[...]).astype(BF16)
    acc = None
    for lo, hi in FF_CHUNKS:
        gate = jnp.dot(hb, w1_ref[:, lo:hi], preferred_element_type=F32)
        up = jnp.dot(hb, w1_ref[:, D_FF + lo:D_FF + hi], preferred_element_type=F32)
        act = (gate * jax.nn.sigmoid(gate) * up).astype(BF16)
        part = jnp.dot(act, w2_ref[lo:hi, :], preferred_element_type=F32)
        acc = part if acc is None else acc + part
    o_ref[...] = x1 + _rms(acc, gpo_ref[...])


def _post_mixer(x, hn, om, attn, layer, prm, tm, name):
    T = x.shape[0]
    row = lambda w: pl.BlockSpec((tm, w), lambda i: (i, 0))
    lyr = lambda shape: _resident((None,) + shape, (layer, 0, 0))
    return pl.pallas_call(
        _post_mixer_kernel,
        grid=(T // tm,),
        in_specs=[row(D_MODEL), row(M_WIDTH), row(M_WIDTH), row(A_WIDTH), lyr((1, M_WIDTH)),
                  lyr((D_MODEL, D_MODEL)), lyr((1, D_MODEL)), lyr((1, D_MODEL)), lyr((D_MODEL, 2 * D_FF)),
                  lyr((D_FF, D_MODEL)), lyr((1, D_MODEL))],
        out_specs=row(D_MODEL),
        out_shape=jax.ShapeDtypeStruct((T, D_MODEL), F32),
        compiler_params=pltpu.CompilerParams(dimension_semantics=("arbitrary",), vmem_limit_bytes=VMEM_LIMIT),
        name=name,
    )(x, hn, om, attn, prm["g_mlstm_head"], prm["w_out"], prm["g_post_mix"], prm["g_pre_ffn"], prm["w_ffn_in"],
      prm["w_ffn_out"], prm["g_post_ffn"])


def _prepare_params(g_pre_mix, g_post_mix, g_pre_ffn, g_post_ffn, w_in, b_gates, g_mlstm_head, g_q_norm, w_uq,
                    g_kv_norm, w_ukv, w_out, w_ffn_in, w_ffn_out):
    depth = w_in.shape[0]
    sizes = (M_HEADS * M_DK, M_HEADS * M_DK, M_WIDTH, M_WIDTH, 2 * M_HEADS, Q_RANK, KV_RANK, ROPE_DIM)
    offs = [0]
    for s in sizes:
        offs.append(offs[-1] + s)
    part = lambda i: w_in[:, :, offs[i]:offs[i + 1]]
    zeros = lambda w: jnp.zeros((depth, D_MODEL, w), w_in.dtype)
    w_in_r = jnp.concatenate(
        [part(0), part(1), part(2), part(3), part(5), part(6),
         zeros(ROPE_LANE0), part(7), zeros(LANES - ROPE_LANE0 - ROPE_DIM),
         part(4), zeros(LANES - 2 * M_HEADS)], axis=-1).astype(BF16)
    w_uq_r = jnp.pad(w_uq.reshape(depth, Q_RANK, A_HEADS, NOPE_DIM + ROPE_DIM),
                     ((0, 0), (0, 0), (0, 0), (0, HEAD_SLOT - NOPE_DIM - ROPE_DIM)))
    w_uq_r = w_uq_r.reshape(depth, Q_RANK, QK_WIDTH).astype(BF16)
    w_ukv4 = w_ukv.reshape(depth, KV_RANK, A_HEADS, NOPE_DIM + V_DIM)
    w_uk4 = jnp.pad(w_ukv4[..., :NOPE_DIM], ((0, 0), (0, 0), (0, 0), (0, HEAD_SLOT - NOPE_DIM)))
    w_uk_r = w_uk4.reshape(depth, KV_RANK, QK_WIDTH).astype(BF16)
    w_ukt = w_uk4.transpose(0, 2, 3, 1).astype(BF16)
    w_uv4 = w_ukv4[..., NOPE_DIM:]
    w_uvt = jnp.pad(w_uv4, ((0, 0), (0, 0), (0, 0), (0, V_SLOT - V_DIM)))
    w_uvt = w_uvt.reshape(depth, KV_RANK, VT_ROWS).transpose(0, 2, 1).astype(BF16)
    w_uvs = jnp.stack([jnp.pad(w_uv4[:, :, h, :], ((0, 0), (0, 0), ((h % 2) * V_DIM, (1 - h % 2) * V_DIM)))
                       for h in range(A_HEADS)], axis=1).astype(BF16)
    row = lambda g: g.astype(F32)[:, None, :]
    return {
        "g_pre_mix": row(g_pre_mix), "g_post_mix": row(g_post_mix), "g_pre_ffn": row(g_pre_ffn),
        "g_post_ffn": row(g_post_ffn), "g_mlstm_head": row(g_mlstm_head), "g_q_norm": row(g_q_norm),
        "g_kv_norm": row(g_kv_norm),
        "b_gates": jnp.pad(b_gates.astype(F32), ((0, 0), (0, LANES - 2 * M_HEADS)))[:, None, :],
        "w_in": w_in_r, "w_uq": w_uq_r, "w_uk": w_uk_r, "w_ukt": w_ukt, "w_uvt": w_uvt, "w_uvs": w_uvs,
        "w_out": w_out.astype(BF16), "w_ffn_in": w_ffn_in.astype(BF16), "w_ffn_out": w_ffn_out.astype(BF16),
    }


def _rope_tables(pos):
    inv = ROPE_THETA ** (-jnp.arange(ROPE_HALF, dtype=F32) / ROPE_HALF)
    ang = pos.astype(F32)[:, None] * inv[None, :]
    cos, sin = jnp.cos(ang), jnp.sin(ang)
    n = pos.shape[0]
    z = lambda w: jnp.zeros((n, w), F32)
    tail = LANES - ROPE_LANE0 - ROPE_DIM
    ra = jnp.concatenate([jnp.ones((n, ROPE_LANE0), F32), cos, cos, z(tail)], axis=-1)
    rbm = jnp.concatenate([z(ROPE_LANE0), -sin, z(ROPE_HALF), z(tail)], axis=-1)
    rbp = jnp.concatenate([z(ROPE_LANE0), z(ROPE_HALF), sin, z(tail)], axis=-1)
    return ra, rbm, rbp


def _krope_placement():
    r = lax.broadcasted_iota(jnp.int32, (ROPE_DIM, HEAD_SLOT), 0)
    c = lax.broadcasted_iota(jnp.int32, (ROPE_DIM, HEAD_SLOT), 1)
    return (c == (r + ROPE_LANE0)).astype(BF16)


TM_PROMPT = 512
TM_SAMPLE = 256
MLSTM_SUB = 8


def kernel(x_prompt, x_sample, cache_ckv, cache_krope, state_mlstm_C, state_mlstm_n, state_mlstm_m,
           g_pre_mix, g_post_mix, g_pre_ffn, g_post_ffn, w_in, b_gates, g_mlstm_head,
           g_q_norm, w_uq, g_kv_norm, w_ukv, w_out, w_ffn_in, w_ffn_out):
    B, S, _ = x_prompt.shape
    Bs, Ls, _ = x_sample.shape
    depth, _, past, _ = cache_ckv.shape
    assert S % TM_PROMPT == 0 and TM_PROMPT % KV_BLOCK == 0 and KV_BLOCK % CHUNK == 0
    assert TM_SAMPLE % Ls == 0 and TM_SAMPLE % KV_BLOCK == 0 and (Bs * Ls) % TM_SAMPLE == 0

    prm = _prepare_params(g_pre_mix, g_post_mix, g_pre_ffn, g_post_ffn, w_in, b_gates, g_mlstm_head, g_q_norm,
                          w_uq, g_kv_norm, w_ukv, w_out, w_ffn_in, w_ffn_out)
    rope_p = _rope_tables(jnp.arange(S))
    rope_s = _rope_tables(past + jnp.arange(TM_SAMPLE) % Ls)
    place = _krope_placement()

    xp = x_prompt.reshape(B * S, D_MODEL)
    xs = x_sample.reshape(Bs * Ls, D_MODEL)
    zero_st = jnp.zeros((B, PAIRS, LANES, ST_COLS), F32)
    zero_m = jnp.zeros((B, 1, M_WIDTH), F32)
    m_in = jnp.repeat(state_mlstm_m.astype(F32), M_DV, axis=-1)[:, :, None, :]

    outs_p = {k: [] for k in ("ckv", "kr", "C", "n", "m")}
    outs_s = {k: [] for k in ("ckv", "kr", "C", "n", "m")}
    for l in range(depth):
        qm, km, vm, om, gates, qa, ka, vt, ckv, kr = _pre_mixer(
            xp, l, prm, rope_p, TM_PROMPT, S // TM_PROMPT, f"pre_mixer_p{l}")
        hn, st, m = _mlstm(qm, km, vm, gates, zero_st, zero_m, B, S // CHUNK, CHUNK, MLSTM_SUB, f"mlstm_p{l}")
        attn = _attn_prompt(qa, ka, vt, B, S, f"attn_p{l}")
        xp = _post_mixer(xp, hn, om, attn, l, prm, TM_PROMPT, f"post_mixer_p{l}")
        for key, val in zip(("ckv", "kr", "C", "n", "m"), (ckv, kr) + _unpack_state(st) + (m,)):
            outs_p[key].append(val)
        qm, km, vm, om, gates, qa, _, _, ckv, kr = _pre_mixer(
            xs, l, prm, rope_s, TM_SAMPLE, 1, f"pre_mixer_s{l}")
        hn, st, m = _mlstm(qm, km, vm, gates, _pack_state(state_mlstm_C[l], state_mlstm_n[l]), m_in[l],
                           Bs, 1, Ls, 1, f"mlstm_s{l}")
        attn = _attn_sample(qa, ckv, kr, cache_ckv, cache_krope, l, prm, place, f"attn_s{l}")
        xs = _post_mixer(xs, hn, om, attn, l, prm, TM_SAMPLE, f"post_mixer_s{l}")
        for key, val in zip(("ckv", "kr", "C", "n", "m"), (ckv, kr) + _unpack_state(st) + (m,)):
            outs_s[key].append(val)

    def collect(o, nb, ln):
        return (jnp.stack(o["ckv"]).reshape(depth, nb, ln, KV_RANK),
                jnp.stack(o["kr"]).reshape(depth, nb, ln, ROPE_DIM),
                jnp.stack(o["C"]), jnp.stack(o["n"]), jnp.stack(o["m"])[:, :, 0, ::M_DV])

    return ((xp.reshape(B, S, D_MODEL), xs.reshape(Bs, Ls, D_MODEL))
            + collect(outs_p, B, S) + collect(outs_s, Bs, Ls))
```

```python
import functools

import jax
import jax.numpy as jnp
from jax import lax
from jax.experimental import pallas as pl
from jax.experimental.pallas import tpu as pltpu

F32 = jnp.float32
BF16 = jnp.bfloat16

D_MODEL = 1024
DEPTH = 4
CHUNK = 64
M_HEADS = 8
M_DK = 64
M_DV = 64
M_WIDTH = M_HEADS * M_DV
A_HEADS = 8
NOPE_DIM = 64
ROPE_DIM = 32
ROPE_HALF = ROPE_DIM // 2
V_DIM = 64
A_WIDTH = A_HEADS * V_DIM
Q_RANK = 256
KV_RANK = 256
ROPE_THETA = 10000.0
D_FF = 2816
RMS_EPS = 1e-6
ATTN_SCALE = (NOPE_DIM + ROPE_DIM) ** -0.5

LANES = 128
HEAD_SLOT = LANES
QK_WIDTH = A_HEADS * HEAD_SLOT
V_SLOT = LANES
VT_ROWS = A_HEADS * V_SLOT
ACC_ROWS = V_DIM + 16
LOG2E = 1.4426950408889634
ROPE_LANE0 = NOPE_DIM

COL_QM = 0
COL_KM = COL_QM + M_HEADS * M_DK
COL_VM = COL_KM + M_HEADS * M_DK
COL_OM = COL_VM + M_WIDTH
COL_CQ = COL_OM + M_WIDTH
COL_CKV = COL_CQ + Q_RANK
COL_KR = COL_CKV + KV_RANK
COL_G = COL_KR + LANES
D_IN_PAD = COL_G + LANES

FF_CHUNKS = ((0, 768), (768, 1792), (1792, 2816))
KV_BLOCK = 256
KEY_SUB = 128

VMEM_LIMIT = 56 * 1024 * 1024

NT_DIMS = (((1,), (1,)), ((), ()))
TN_DIMS = (((0,), (0,)), ((), ()))


def _resident(block_shape, index):
    return pl.BlockSpec(block_shape, lambda *_: index, pipeline_mode=pl.Buffered(1))


def _rms(x, g):
    return x * lax.rsqrt(jnp.mean(x * x, axis=-1, keepdims=True) + RMS_EPS) * g


def _rope(x, ra, rbm, rbp):
    return x * ra + pltpu.roll(x, LANES - ROPE_HALF, axis=1) * rbm + pltpu.roll(x, ROPE_HALF, axis=1) * rbp


def _pre_mixer_kernel(x_ref, g_ref, win_ref, bg_ref, gq_ref, wuq_ref, gkv_ref, wuk_ref, wuvt_ref,
                      ra_ref, rbm_ref, rbp_ref, ckv_all_ref, kr_all_ref,
                      qm_ref, km_ref, vm_ref, om_ref, gate_ref, qa_ref, ka_ref, vt_ref, ckv_ref, kr_ref):
    del ckv_all_ref, kr_all_ref
    xb = _rms(x_ref[...], g_ref[...]).astype(BF16)

    def proj(col, width):
        return jnp.dot(xb, win_ref[:, col:col + width], preferred_element_type=F32)

    qm_ref[...] = proj(COL_QM, M_HEADS * M_DK).astype(BF16)
    km_ref[...] = (proj(COL_KM, M_HEADS * M_DK) * (M_DK ** -0.5)).astype(BF16)
    vm_ref[...] = proj(COL_VM, M_WIDTH).astype(BF16)
    om_ref[...] = proj(COL_OM, M_WIDTH)

    gates = proj(COL_G, LANES) + bg_ref[...]
    lane = lax.broadcasted_iota(jnp.int32, gates.shape, 1)
    log_f = jnp.minimum(gates, 0.0) - jnp.log1p(jnp.exp(-jnp.abs(gates)))
    gate_ref[...] = jnp.where(lane < M_HEADS, gates, log_f)

    ra, rbm, rbp = ra_ref[...], rbm_ref[...], rbp_ref[...]

    cqn = _rms(proj(COL_CQ, Q_RANK), gq_ref[...]).astype(BF16)
    q = jnp.dot(cqn, wuq_ref[...], preferred_element_type=F32)
    for h in range(A_HEADS):
        sl = slice(h * HEAD_SLOT, (h + 1) * HEAD_SLOT)
        qa_ref[:, sl] = (_rope(q[:, sl], ra, rbm, rbp) * (ATTN_SCALE * LOG2E)).astype(BF16)

    ckvn = _rms(proj(COL_CKV, KV_RANK), gkv_ref[...])
    ckv_ref[...] = ckvn
    ckvb = ckvn.astype(BF16)
    kr = _rope(proj(COL_KR, LANES), ra, rbm, rbp)
    kr_ref[...] = kr[:, ROPE_LANE0:ROPE_LANE0 + ROPE_DIM]
    kmat = jnp.dot(ckvb, wuk_ref[...], preferred_element_type=F32)
    for h in range(A_HEADS):
        sl = slice(h * HEAD_SLOT, (h + 1) * HEAD_SLOT)
        ka_ref[:, sl] = (kmat[:, sl] + kr).astype(BF16)
    vt = lax.dot_general(wuvt_ref[...], ckvb, NT_DIMS, preferred_element_type=F32)
    slot_row = lax.broadcasted_iota(jnp.int32, vt.shape, 0) % V_SLOT
    vt = jnp.where(slot_row == V_DIM, 1.0, vt).astype(BF16)
    for s in range(vt_ref.shape[0]):
        vt_ref[s] = vt[:, s * KV_BLOCK:(s + 1) * KV_BLOCK]


def _pre_mixer(x, layer, prm, rope_tabs, tm, n_rope_blocks, ckv_all, kr_all, name):
    T = x.shape[0]
    ra, rbm, rbp = rope_tabs
    row = lambda w: pl.BlockSpec((tm, w), lambda i: (i, 0))
    rope_spec = pl.BlockSpec((tm, LANES), lambda i: (i % n_rope_blocks, 0))
    lyr = lambda shape: _resident((None,) + shape, (layer, 0, 0))
    untouched = pl.BlockSpec(memory_space=pl.ANY)
    slab = lambda w: pl.BlockSpec((None, tm, w), lambda i: (layer, i, 0))
    kvb = tm // KV_BLOCK
    out_specs = [row(M_HEADS * M_DK), row(M_HEADS * M_DK), row(M_WIDTH), row(M_WIDTH), row(LANES),
                 row(QK_WIDTH), row(QK_WIDTH), pl.BlockSpec((kvb, VT_ROWS, KV_BLOCK), lambda i: (i, 0, 0)),
                 slab(KV_RANK), slab(ROPE_DIM)]
    tok = lambda w, d: jax.ShapeDtypeStruct((T, w), d)
    out_shape = [tok(M_HEADS * M_DK, BF16), tok(M_HEADS * M_DK, BF16), tok(M_WIDTH, BF16), tok(M_WIDTH, F32),
                 tok(LANES, F32), tok(QK_WIDTH, BF16), tok(QK_WIDTH, BF16),
                 jax.ShapeDtypeStruct((T // KV_BLOCK, VT_ROWS, KV_BLOCK), BF16),
                 jax.ShapeDtypeStruct(ckv_all.shape, F32), jax.ShapeDtypeStruct(kr_all.shape, F32)]
    in_specs = [row(D_MODEL), lyr((1, D_MODEL)), lyr((D_MODEL, D_IN_PAD)), lyr((1, LANES)),
                lyr((1, Q_RANK)), lyr((Q_RANK, QK_WIDTH)), lyr((1, KV_RANK)),
                lyr((KV_RANK, QK_WIDTH)), lyr((VT_ROWS, KV_RANK)), rope_spec, rope_spec, rope_spec,
                untouched, untouched]
    return pl.pallas_call(
        _pre_mixer_kernel,
        grid=(T // tm,),
        in_specs=in_specs,
        out_specs=out_specs,
        out_shape=out_shape,
        input_output_aliases={len(in_specs) - 2: len(out_specs) - 2, len(in_specs) - 1: len(out_specs) - 1},
        compiler_params=pltpu.CompilerParams(dimension_semantics=("arbitrary",), vmem_limit_bytes=VMEM_LIMIT),
        name=name,
    )(x, prm["g_pre_mix"], prm["w_in"], prm["b_gates"], prm["g_q_norm"], prm["w_uq"], prm["g_kv_norm"],
      prm["w_uk"], prm["w_uvt"], ra, rbm, rbp, ckv_all, kr_all)


PAIRS = M_HEADS // 2
ST_COLS = LANES


def _split_bf16(x, parts):
    out = []
    for _ in range(parts - 1):
        piece = x.astype(BF16)
        out.append(piece)
        x = x - piece.astype(F32)
    out.append(x.astype(BF16))
    return out


def _dot_pieces(lhs, x, parts, lhs_first):
    acc = None
    for piece in _split_bf16(x, parts):
        d = (jnp.dot(lhs, piece, preferred_element_type=F32) if lhs_first
             else jnp.dot(piece, lhs, preferred_element_type=F32))
        acc = d if acc is None else acc + d
    return acc


def _cummax_rows(x, row):
    d = 1
    while d < x.shape[0]:
        x = jnp.where(row >= d, jnp.maximum(x, pltpu.roll(x, d, axis=0)), x)
        d *= 2
    return x


def _mlstm_kernel(q_ref, k_ref, v_ref, g_ref, st0_ref, m0_ref, h_ref, st_ref, m_ref, *, L, n_sub):
    @pl.when(pl.program_id(1) == 0)
    def _():
        st_ref[...] = st0_ref[...]
        m_ref[...] = m0_ref[...]

    row_w = lax.broadcasted_iota(jnp.int32, (L, M_WIDTH), 0)
    row = lax.broadcasted_iota(jnp.int32, (L, LANES), 0)
    lane = lax.broadcasted_iota(jnp.int32, (L, LANES), 1)
    first_head = lane < M_DK
    key = lane % M_DK
    causal = key <= row
    diagonal = key == row
    tri = (lax.broadcasted_iota(jnp.int32, (L, L), 1) <= lax.broadcasted_iota(jnp.int32, (L, L), 0)).astype(BF16)
    expand = (lax.broadcasted_iota(jnp.int32, (LANES, 2 * M_WIDTH), 0)
              == lax.broadcasted_iota(jnp.int32, (LANES, 2 * M_WIDTH), 1) // M_DV).astype(BF16)
    gate_lane = lax.broadcasted_iota(jnp.int32, (L, LANES), 1)
    st_row_first = lax.broadcasted_iota(jnp.int32, (LANES, LANES), 0) < M_DK
    ones_bd = (st_row_first == (lax.broadcasted_iota(jnp.int32, (LANES, LANES), 1) < M_DK)).astype(BF16)
    pairs = range(PAIRS)
    lanes_of = lambda p: slice(p * LANES, (p + 1) * LANES)

    def split_heads(x, fill_first=0, fill_second=0):
        return (jnp.where(first_head, x, jnp.full_like(x, fill_first)),
                jnp.where(first_head, jnp.full_like(x, fill_second), x))

    def stack_heads(x):
        parts = []
        for half in split_heads(x):
            parts.append(half)
            if L < M_DK:
                parts.append(jnp.zeros((M_DK - L, LANES), x.dtype))
        return jnp.concatenate(parts, axis=0)

    def stage_gates(c):
        rows = slice(c * L, (c + 1) * L)
        gates = g_ref[rows, :]
        csum = _dot_pieces(tri, gates, 3, True)
        wide = _dot_pieces(expand, jnp.where(gate_lane < M_HEADS, gates, csum), 2, False)
        b = wide[:, M_WIDTH:]
        a = wide[:, :M_WIDTH] - b
        return dict(rows=rows, b=b, a=a, a_cummax=_cummax_rows(a, row_w))

    def stage_issue(ck):
        rows, b, a = ck["rows"], ck["b"], ck["a"]
        m0 = m_ref[...]
        mx = jnp.maximum(ck["a_cummax"], m0)
        mx_last = mx[L - 1:L, :]
        m_ref[...] = b[L - 1:L, :] + mx_last
        ck.update(mx=mx, w_inter=jnp.exp(m0 - mx), clamp=jnp.exp(-(b + mx)), decay=jnp.exp(m0 - mx_last))
        w_key = jnp.exp(a - mx_last)
        qs = [q_ref[rows, lanes_of(p)] for p in pairs]
        ks = [k_ref[rows, lanes_of(p)] for p in pairs]
        ck["vs"] = vs = [v_ref[rows, lanes_of(p)] for p in pairs]
        ck["scores"] = [lax.dot_general(qs[p], stack_heads(ks[p]), NT_DIMS, preferred_element_type=F32)
                        for p in pairs]
        ck["inter"] = [jnp.dot(jnp.concatenate(split_heads(qs[p]), axis=0), st_ref[p].astype(BF16),
                               preferred_element_type=F32) for p in pairs]
        upd = []
        for p in pairs:
            kw = (ks[p].astype(F32) * w_key[:, lanes_of(p)]).astype(BF16)
            upd.append(lax.dot_general(jnp.concatenate(split_heads(kw), axis=0),
                                       jnp.concatenate(split_heads(vs[p], 1, 1), axis=0), TN_DIMS,
                                       preferred_element_type=F32))
        ck["upd"] = upd

    def stage_state(ck):
        for p in pairs:
            dec = ck["decay"][:, lanes_of(p)]
            dec_rows = jnp.where(st_row_first, dec[:, 0:1], dec[:, M_DK:M_DK + 1])
            st_ref[p] = dec_rows * st_ref[p] + ck["upd"][p]

    def stage_intra(ck):
        intra = []
        for p in pairs:
            a_row = jnp.sum(jnp.where(diagonal, ck["a"][:, lanes_of(p)], 0.0), axis=0, keepdims=True)
            d_m = jnp.where(causal, a_row, -jnp.inf)
            s_mat = (ck["scores"][p] * jnp.exp(d_m - ck["mx"][:, lanes_of(p)])).astype(BF16)
            intra.append(jnp.dot(s_mat, jnp.concatenate([stack_heads(ck["vs"][p]), ones_bd], axis=1),
                                 preferred_element_type=F32))
        ck["intra"] = intra

    def stage_cell(ck):
        cells = []
        for p in pairs:
            first, second = ck["inter"][p][:L], ck["inter"][p][L:]
            q_c = jnp.where(first_head, first, second)
            q_n = pltpu.roll(jnp.where(first_head, second, first), M_DK, axis=1)
            wi = ck["w_inter"][:, lanes_of(p)]
            num = wi * q_c + ck["intra"][p][:, :LANES]
            den = wi * q_n + ck["intra"][p][:, LANES:]
            h_cell = num / jnp.maximum(jnp.abs(den), ck["clamp"][:, lanes_of(p)])
            cells.append((h_cell, _dot_pieces(ones_bd, h_cell * h_cell, 2, False)))
        ck["cells"] = cells

    def stage_out(ck):
        for p in pairs:
            h_cell, sum_sq = ck["cells"][p]
            h_ref[ck["rows"], lanes_of(p)] = h_cell * lax.rsqrt(sum_sq * (1.0 / M_DV) + RMS_EPS)

    chunks = [stage_gates(c) for c in range(n_sub)]
    for t in range(n_sub + 3):
        live = lambda c: 0 <= c < n_sub
        if live(t - 1):
            stage_state(chunks[t - 1])
        if live(t):
            stage_issue(chunks[t])
        if live(t - 1):
            stage_intra(chunks[t - 1])
        if live(t - 2):
            stage_cell(chunks[t - 2])
        if live(t - 3):
            stage_out(chunks[t - 3])


def _mlstm(qm, km, vm, gates, st0, m0, n_streams, n_chunks, L, n_sub, name):
    T = qm.shape[0]
    steps = n_chunks // n_sub
    tok = pl.BlockSpec((L * n_sub, M_WIDTH), lambda b, c: (b * steps + c, 0))
    gate_spec = pl.BlockSpec((L * n_sub, LANES), lambda b, c: (b * steps + c, 0))
    st_spec = pl.BlockSpec((None, PAIRS, LANES, ST_COLS), lambda b, c: (b, 0, 0, 0))
    m_spec = pl.BlockSpec((None, 1, M_WIDTH), lambda b, c: (b, 0, 0))
    return pl.pallas_call(
        functools.partial(_mlstm_kernel, L=L, n_sub=n_sub),
        grid=(n_streams, steps),
        in_specs=[tok, tok, tok, gate_spec, st_spec, m_spec],
        out_specs=[tok, st_spec, m_spec],
        out_shape=[jax.ShapeDtypeStruct((T, M_WIDTH), F32),
                   jax.ShapeDtypeStruct((n_streams, PAIRS, LANES, ST_COLS), F32),
                   jax.ShapeDtypeStruct((n_streams, 1, M_WIDTH), F32)],
        compiler_params=pltpu.CompilerParams(dimension_semantics=("arbitrary", "arbitrary")),
        name=name,
    )(qm, km, vm, gates, st0, m0)


def _pack_state(C, n):
    B = C.shape[0]
    c4 = C.astype(F32).reshape(B, PAIRS, 2, M_DK, M_DV)
    n4 = jnp.broadcast_to(n.astype(F32).reshape(B, PAIRS, 2, M_DK, 1), (B, PAIRS, 2, M_DK, M_DV))
    return jnp.concatenate([jnp.concatenate([c4[:, :, 0], n4[:, :, 0]], axis=-1),
                            jnp.concatenate([n4[:, :, 1], c4[:, :, 1]], axis=-1)], axis=-2)


def _unpack_state(st):
    B = st.shape[0]
    half = lambda hh: slice(hh * M_DK, (hh + 1) * M_DK)
    C = jnp.stack([st[:, :, half(hh), half(hh)] for hh in (0, 1)], axis=2).reshape(B, M_HEADS, M_DK, M_DV)
    n = jnp.stack([st[:, :, half(hh), (1 - hh) * M_DK] for hh in (0, 1)], axis=2).reshape(B, M_HEADS, M_DK)
    return C, n


def _attn_prompt_kernel(q_ref, k_ref, vt_ref, o_ref, acc_ref, m_ref):
    i = pl.program_id(1)
    tq = q_ref.shape[0]
    acc_ref[...] = jnp.zeros_like(acc_ref)
    m_ref[...] = jnp.full_like(m_ref, -jnp.inf)
    key = lax.broadcasted_iota(jnp.int32, (KEY_SUB, tq), 0)
    qry = lax.broadcasted_iota(jnp.int32, (KEY_SUB, tq), 1)
    items = [(h, sub) for h in range(A_HEADS) for sub in range(tq // KEY_SUB)]

    def block(j, masked):
        def scores(h, sub):
            hsl = slice(h * HEAD_SLOT, (h + 1) * HEAD_SLOT)
            rows = pl.ds(pl.multiple_of(j * tq + sub * KEY_SUB, KEY_SUB), KEY_SUB)
            return lax.dot_general(k_ref[rows, hsl], q_ref[:, hsl], NT_DIMS, preferred_element_type=F32)

        s_next = scores(*items[0])
        for n, (h, sub) in enumerate(items):
            s_t = s_next
            if n + 1 < len(items):
                s_next = scores(*items[n + 1])
            if masked:
                s_t = jnp.where((key + sub * KEY_SUB) // CHUNK <= qry // CHUNK, s_t, -jnp.inf)
            m_old = m_ref[h:h + 1, :]
            m_new = jnp.maximum(m_old, jnp.max(s_t, axis=0, keepdims=True))
            m_ref[h:h + 1, :] = m_new
            p_t = jnp.exp2(s_t - m_new).astype(BF16)
            vt = vt_ref[j, h * V_SLOT:(h + 1) * V_SLOT, sub * KEY_SUB:(sub + 1) * KEY_SUB]
            pv = jnp.dot(vt, p_t, preferred_element_type=F32)[:ACC_ROWS]
            acc_ref[h] = jnp.exp2(m_old - m_new) * acc_ref[h] + pv

    def body(j, carry):
        block(j, False)
        return carry

    lax.fori_loop(0, i, body, 0)
    block(i, True)
    outs = []
    for h in range(A_HEADS):
        acc = acc_ref[h]
        outs.append(acc[:V_DIM, :] / acc[V_DIM:V_DIM + 1, :])
    o_ref[...] = jnp.concatenate(outs, axis=0).T.astype(BF16)


def _attn_prompt(qa, ka, vt, n_streams, seq, name):
    T = qa.shape[0]
    tq = KV_BLOCK
    nq = seq // tq
    return pl.pallas_call(
        _attn_prompt_kernel,
        grid=(n_streams, nq),
        in_specs=[pl.BlockSpec((tq, QK_WIDTH), lambda b, i: (b * nq + i, 0)),
                  pl.BlockSpec((seq, QK_WIDTH), lambda b, i: (b, 0)),
                  pl.BlockSpec((nq, VT_ROWS, tq), lambda b, i: (b, 0, 0))],
        out_specs=pl.BlockSpec((tq, A_WIDTH), lambda b, i: (b * nq + i, 0)),
        out_shape=jax.ShapeDtypeStruct((T, A_WIDTH), BF16),
        scratch_shapes=[pltpu.VMEM((A_HEADS, ACC_ROWS, tq), F32), pltpu.VMEM((A_HEADS, tq), F32)],
        compiler_params=pltpu.CompilerParams(dimension_semantics=("arbitrary", "arbitrary"),
                                             vmem_limit_bytes=VMEM_LIMIT),
        name=name,
    )(qa, ka, vt)


def _attn_sample_kernel(q_ref, ckvn_ref, krn_ref, ckvp_ref, krp_ref, wukt_ref, wuvs_ref, place_ref, o_ref):
    L = q_ref.shape[0]
    q_heads = [q_ref[:, h * HEAD_SLOT:(h + 1) * HEAD_SLOT] for h in range(A_HEADS)]
    q_slot = jnp.concatenate(q_heads, axis=0)
    q_lat = jnp.concatenate([jnp.dot(q_heads[h], wukt_ref[h], preferred_element_type=F32)
                             for h in range(A_HEADS)], axis=0).astype(BF16)

    def scores(ckv, kr):
        ckvb = ckv.astype(BF16)
        kr_slot = jnp.dot(kr.astype(BF16), place_ref[...], preferred_element_type=F32).astype(BF16)
        s = (lax.dot_general(q_lat, ckvb, NT_DIMS, preferred_element_type=F32)
             + lax.dot_general(q_slot, kr_slot, NT_DIMS, preferred_element_type=F32))
        return s, ckvb

    s_p, ckvb_p = scores(ckvp_ref[...], krp_ref[...])
    s_n, ckvb_n = scores(ckvn_ref[...], krn_ref[...])
    m = jnp.maximum(jnp.max(s_p, axis=-1, keepdims=True), jnp.max(s_n, axis=-1, keepdims=True))
    p_p = jnp.exp2(s_p - m)
    p_n = jnp.exp2(s_n - m)
    denom = jnp.sum(p_p, axis=-1, keepdims=True) + jnp.sum(p_n, axis=-1, keepdims=True)
    o_lat = (jnp.dot(p_p.astype(BF16), ckvb_p, preferred_element_type=F32)
             + jnp.dot(p_n.astype(BF16), ckvb_n, preferred_element_type=F32)) / denom
    o_lat = o_lat.astype(BF16)
    for pair in range(A_HEADS // 2):
        h0, h1 = 2 * pair, 2 * pair + 1
        out = (jnp.dot(o_lat[h0 * L:(h0 + 1) * L], wuvs_ref[h0], preferred_element_type=F32)
               + jnp.dot(o_lat[h1 * L:(h1 + 1) * L], wuvs_ref[h1], preferred_element_type=F32))
        o_ref[:, pair * LANES:(pair + 1) * LANES] = out.astype(BF16)


def _attn_sample(qa, ckv_new, kr_new, cache_ckv, cache_krope, layer, prm, place, name):
    T = qa.shape[0]
    n_streams, past = cache_ckv.shape[1], cache_ckv.shape[2]
    L = T // n_streams
    tok = lambda w: pl.BlockSpec((L, w), lambda b: (b, 0))
    new = lambda w: pl.BlockSpec((None, L, w), lambda b: (layer, b, 0))
    return pl.pallas_call(
        _attn_sample_kernel,
        grid=(n_streams,),
        in_specs=[tok(QK_WIDTH), new(KV_RANK), new(ROPE_DIM),
                  pl.BlockSpec((None, None, past, KV_RANK), lambda b: (layer, b, 0, 0)),
                  pl.BlockSpec((None, None, past, ROPE_DIM), lambda b: (layer, b, 0, 0)),
                  _resident((None, A_HEADS, HEAD_SLOT, KV_RANK), (layer, 0, 0, 0)),
                  _resident((None, A_HEADS, KV_RANK, LANES), (layer, 0, 0, 0)),
                  _resident((ROPE_DIM, HEAD_SLOT), (0, 0))],
        out_specs=tok(A_WIDTH),
        out_shape=jax.ShapeDtypeStruct((T, A_WIDTH), BF16),
        compiler_params=pltpu.CompilerParams(dimension_semantics=("arbitrary",), vmem_limit_bytes=VMEM_LIMIT),
        name=name,
    )(qa, ckv_new, kr_new, cache_ckv, cache_krope, prm["w_ukt"], prm["w_uvs"], place)


def _post_mixer_kernel(x_ref, hn_ref, om_ref, at_ref, gh_ref, wo_ref, gpm_ref, gpf_ref, w1_ref, w2_ref, gpo_ref,
                       o_ref):
    tm = x_ref.shape[0]
    groups = [slice(g * tm // ROW_GROUPS, (g + 1) * tm // ROW_GROUPS) for g in range(ROW_GROUPS)]
    mix = []
    for r in groups:
        h_m = jax.nn.sigmoid(om_ref[r, :]) * (hn_ref[r, :] * gh_ref[...])
        mix.append(jnp.dot(jnp.concatenate([h_m.astype(BF16), at_ref[r, :]], axis=-1), wo_ref[...],
                           preferred_element_type=F32))
    x1 = [x_ref[r, :] + _rms(m, gpm_ref[...]) for r, m in zip(groups, mix)]
    hb = [_rms(x, gpf_ref[...]).astype(BF16) for x in x1]
    acc = [None] * ROW_GROUPS
    for lo, hi in FF_CHUNKS:
        gate_up = [(jnp.dot(h, w1_ref[:, lo:hi], preferred_element_type=F32),
                    jnp.dot(h, w1_ref[:, D_FF + lo:D_FF + hi], preferred_element_type=F32)) for h in hb]
        for g, (gate, up) in enumerate(gate_up):
            act = (gate * jax.nn.sigmoid(gate) * up).astype(BF16)
            part = jnp.dot(act, w2_ref[lo:hi, :], preferred_element_type=F32)
            acc[g] = part if acc[g] is None else acc[g] + part
    for r, x, a in zip(groups, x1, acc):
        o_ref[r, :] = x + _rms(a, gpo_ref[...])


def _post_mixer(x, hn, om, attn, layer, prm, tm, name):
    T = x.shape[0]
    row = lambda w: pl.BlockSpec((tm, w), lambda i: (i, 0))
    lyr = lambda shape: _resident((None,) + shape, (layer, 0, 0))
    return pl.pallas_call(
        _post_mixer_kernel,
        grid=(T // tm,),
        in_specs=[row(D_MODEL), row(M_WIDTH), row(M_WIDTH), row(A_WIDTH), lyr((1, M_WIDTH)),
                  lyr((D_MODEL, D_MODEL)), lyr((1, D_MODEL)), lyr((1, D_MODEL)), lyr((D_MODEL, 2 * D_FF)),
                  lyr((D_FF, D_MODEL)), lyr((1, D_MODEL))],
        out_specs=row(D_MODEL),
        out_shape=jax.ShapeDtypeStruct((T, D_MODEL), F32),
        compiler_params=pltpu.CompilerParams(dimension_semantics=("arbitrary",), vmem_limit_bytes=VMEM_LIMIT),
        name=name,
    )(x, hn, om, attn, prm["g_mlstm_head"], prm["w_out"], prm["g_post_mix"], prm["g_pre_ffn"], prm["w_ffn_in"],
      prm["w_ffn_out"], prm["g_post_ffn"])


def _prepare_params(g_pre_mix, g_post_mix, g_pre_ffn, g_post_ffn, w_in, b_gates, g_mlstm_head, g_q_norm, w_uq,
                    g_kv_norm, w_ukv, w_out, w_ffn_in, w_ffn_out):
    depth = w_in.shape[0]
    sizes = (M_HEADS * M_DK, M_HEADS * M_DK, M_WIDTH, M_WIDTH, 2 * M_HEADS, Q_RANK, KV_RANK, ROPE_DIM)
    offs = [0]
    for s in sizes:
        offs.append(offs[-1] + s)
    part = lambda i: w_in[:, :, offs[i]:offs[i + 1]]
    zeros = lambda w: jnp.zeros((depth, D_MODEL, w), w_in.dtype)
    w_in_r = jnp.concatenate(
        [part(0), part(1), part(2), part(3), part(5), part(6),
         zeros(ROPE_LANE0), part(7), zeros(LANES - ROPE_LANE0 - ROPE_DIM),
         part(4), zeros(LANES - 2 * M_HEADS)], axis=-1).astype(BF16)
    w_uq_r = jnp.pad(w_uq.reshape(depth, Q_RANK, A_HEADS, NOPE_DIM + ROPE_DIM),
                     ((0, 0), (0, 0), (0, 0), (0, HEAD_SLOT - NOPE_DIM - ROPE_DIM)))
    w_uq_r = w_uq_r.reshape(depth, Q_RANK, QK_WIDTH).astype(BF16)
    w_ukv4 = w_ukv.reshape(depth, KV_RANK, A_HEADS, NOPE_DIM + V_DIM)
    w_uk4 = jnp.pad(w_ukv4[..., :NOPE_DIM], ((0, 0), (0, 0), (0, 0), (0, HEAD_SLOT - NOPE_DIM)))
    w_uk_r = w_uk4.reshape(depth, KV_RANK, QK_WIDTH).astype(BF16)
    w_ukt = w_uk4.transpose(0, 2, 3, 1).astype(BF16)
    w_uv4 = w_ukv4[..., NOPE_DIM:]
    w_uvt = jnp.pad(w_uv4, ((0, 0), (0, 0), (0, 0), (0, V_SLOT - V_DIM)))
    w_uvt = w_uvt.reshape(depth, KV_RANK, VT_ROWS).transpose(0, 2, 1).astype(BF16)
    w_uvs = jnp.stack([jnp.pad(w_uv4[:, :, h, :], ((0, 0), (0, 0), ((h % 2) * V_DIM, (1 - h % 2) * V_DIM)))
                       for h in range(A_HEADS)], axis=1).astype(BF16)
    row = lambda g: g.astype(F32)[:, None, :]
    return {
        "g_pre_mix": row(g_pre_mix), "g_post_mix": row(g_post_mix), "g_pre_ffn": row(g_pre_ffn),
        "g_post_ffn": row(g_post_ffn), "g_mlstm_head": row(g_mlstm_head), "g_q_norm": row(g_q_norm),
        "g_kv_norm": row(g_kv_norm),
        "b_gates": jnp.pad(b_gates.astype(F32), ((0, 0), (0, LANES - 2 * M_HEADS)))[:, None, :],
        "w_in": w_in_r, "w_uq": w_uq_r, "w_uk": w_uk_r, "w_ukt": w_ukt, "w_uvt": w_uvt, "w_uvs": w_uvs,
        "w_out": w_out.astype(BF16), "w_ffn_in": w_ffn_in.astype(BF16), "w_ffn_out": w_ffn_out.astype(BF16),
    }


def _rope_tables(pos):
    inv = ROPE_THETA ** (-jnp.arange(ROPE_HALF, dtype=F32) / ROPE_HALF)
    ang = pos.astype(F32)[:, None] * inv[None, :]
    cos, sin = jnp.cos(ang), jnp.sin(ang)
    n = pos.shape[0]
    z = lambda w: jnp.zeros((n, w), F32)
    tail = LANES - ROPE_LANE0 - ROPE_DIM
    ra = jnp.concatenate([jnp.ones((n, ROPE_LANE0), F32), cos, cos, z(tail)], axis=-1)
    rbm = jnp.concatenate([z(ROPE_LANE0), -sin, z(ROPE_HALF), z(tail)], axis=-1)
    rbp = jnp.concatenate([z(ROPE_LANE0), z(ROPE_HALF), sin, z(tail)], axis=-1)
    return ra, rbm, rbp


def _krope_placement():
    r = lax.broadcasted_iota(jnp.int32, (ROPE_DIM, HEAD_SLOT), 0)
    c = lax.broadcasted_iota(jnp.int32, (ROPE_DIM, HEAD_SLOT), 1)
    return (c == (r + ROPE_LANE0)).astype(BF16)


TM_PROMPT = 512
TM_SAMPLE = 256
MLSTM_SUB = 8
ROW_GROUPS = 2


def kernel(x_prompt, x_sample, cache_ckv, cache_krope, state_mlstm_C, state_mlstm_n, state_mlstm_m,
           g_pre_mix, g_post_mix, g_pre_ffn, g_post_ffn, w_in, b_gates, g_mlstm_head,
           g_q_norm, w_uq, g_kv_norm, w_ukv, w_out, w_ffn_in, w_ffn_out):
    B, S, _ = x_prompt.shape
    Bs, Ls, _ = x_sample.shape
    depth, _, past, _ = cache_ckv.shape
    assert S % TM_PROMPT == 0 and TM_PROMPT % KV_BLOCK == 0 and KV_BLOCK % CHUNK == 0
    assert TM_SAMPLE % Ls == 0 and TM_SAMPLE % KV_BLOCK == 0 and (Bs * Ls) % TM_SAMPLE == 0

    prm = _prepare_params(g_pre_mix, g_post_mix, g_pre_ffn, g_post_ffn, w_in, b_gates, g_mlstm_head, g_q_norm,
                          w_uq, g_kv_norm, w_ukv, w_out, w_ffn_in, w_ffn_out)
    rope_p = _rope_tables(jnp.arange(S))
    rope_s = _rope_tables(past + jnp.arange(TM_SAMPLE) % Ls)
    place = _krope_placement()

    xp = x_prompt.reshape(B * S, D_MODEL)
    xs = x_sample.reshape(Bs * Ls, D_MODEL)
    zero_st = jnp.zeros((B, PAIRS, LANES, ST_COLS), F32)
    zero_m = jnp.zeros((B, 1, M_WIDTH), F32)
    m_in = jnp.repeat(state_mlstm_m.astype(F32), M_DV, axis=-1)[:, :, None, :]

    ckv_p, kr_p = jnp.zeros((depth, B * S, KV_RANK), F32), jnp.zeros((depth, B * S, ROPE_DIM), F32)
    ckv_s, kr_s = jnp.zeros((depth, Bs * Ls, KV_RANK), F32), jnp.zeros((depth, Bs * Ls, ROPE_DIM), F32)

    outs_p = {k: [] for k in ("C", "n", "m")}
    outs_s = {k: [] for k in ("C", "n", "m")}
    for l in range(depth):
        qm, km, vm, om, gates, qa, ka, vt, ckv_p, kr_p = _pre_mixer(
            xp, l, prm, rope_p, TM_PROMPT, S // TM_PROMPT, ckv_p, kr_p, f"pre_mixer_p{l}")
        hn, st, m = _mlstm(qm, km, vm, gates, zero_st, zero_m, B, S // CHUNK, CHUNK, MLSTM_SUB, f"mlstm_p{l}")
        attn = _attn_prompt(qa, ka, vt, B, S, f"attn_p{l}")
        xp = _post_mixer(xp, hn, om, attn, l, prm, TM_PROMPT, f"post_mixer_p{l}")
        for key, val in zip(("C", "n", "m"), _unpack_state(st) + (m,)):
            outs_p[key].append(val)
        qm, km, vm, om, gates, qa, _, _, ckv_s, kr_s = _pre_mixer(
            xs, l, prm, rope_s, TM_SAMPLE, 1, ckv_s, kr_s, f"pre_mixer_s{l}")
        hn, st, m = _mlstm(qm, km, vm, gates, _pack_state(state_mlstm_C[l], state_mlstm_n[l]), m_in[l],
                           Bs, 1, Ls, 1, f"mlstm_s{l}")
        attn = _attn_sample(qa, ckv_s, kr_s, cache_ckv, cache_krope, l, prm, place, f"attn_s{l}")
        xs = _post_mixer(xs, hn, om, attn, l, prm, TM_SAMPLE, f"post_mixer_s{l}")
        for key, val in zip(("C", "n", "m"), _unpack_state(st) + (m,)):
            outs_s[key].append(val)

    def collect(ckv, kr, o, nb, ln):
        return (ckv.reshape(depth, nb, ln, KV_RANK), kr.reshape(depth, nb, ln, ROPE_DIM),
                jnp.stack(o["C"]), jnp.stack(o["n"]), jnp.stack(o["m"])[:, :, 0, ::M_DV])

    return ((xp.reshape(B, S, D_MODEL), xs.reshape(Bs, Ls, D_MODEL))
            + collect(ckv_p, kr_p, outs_p, B, S) + collect(ckv_s, kr_s, outs_s, Bs, Ls))
```

```python
import functools

import jax
import jax.numpy as jnp
from jax import lax
from jax.experimental import pallas as pl
from jax.experimental.pallas import tpu as pltpu

F32 = jnp.float32
BF16 = jnp.bfloat16

D_MODEL = 1024
DEPTH = 4
CHUNK = 64
M_HEADS = 8
M_DK = 64
M_DV = 64
M_WIDTH = M_HEADS * M_DV
A_HEADS = 8
NOPE_DIM = 64
ROPE_DIM = 32
ROPE_HALF = ROPE_DIM // 2
V_DIM = 64
A_WIDTH = A_HEADS * V_DIM
Q_RANK = 256
KV_RANK = 256
ROPE_THETA = 10000.0
D_FF = 2816
RMS_EPS = 1e-6
ATTN_SCALE = (NOPE_DIM + ROPE_DIM) ** -0.5

LANES = 128
HEAD_SLOT = LANES
QK_WIDTH = A_HEADS * HEAD_SLOT
V_SLOT = LANES
VT_ROWS = A_HEADS * V_SLOT
ACC_ROWS = V_DIM + 16
LOG2E = 1.4426950408889634
ROPE_LANE0 = NOPE_DIM

COL_QM = 0
COL_KM = COL_QM + M_HEADS * M_DK
COL_VM = COL_KM + M_HEADS * M_DK
COL_OM = COL_VM + M_WIDTH
COL_CQ = COL_OM + M_WIDTH
COL_CKV = COL_CQ + Q_RANK
COL_KR = COL_CKV + KV_RANK
COL_G = COL_KR + LANES
D_IN_PAD = COL_G + LANES

FF_CHUNKS = ((0, 768), (768, 1792), (1792, 2816))
KV_BLOCK = 256
KEY_SUB = 128

VMEM_LIMIT = 56 * 1024 * 1024

NT_DIMS = (((1,), (1,)), ((), ()))
TN_DIMS = (((0,), (0,)), ((), ()))


def _resident(block_shape, index):
    return pl.BlockSpec(block_shape, lambda *_: index, pipeline_mode=pl.Buffered(1))


def _rms(x, g):
    return x * lax.rsqrt(jnp.mean(x * x, axis=-1, keepdims=True) + RMS_EPS) * g


def _rope(x, ra, rbm, rbp):
    return x * ra + pltpu.roll(x, LANES - ROPE_HALF, axis=1) * rbm + pltpu.roll(x, ROPE_HALF, axis=1) * rbp


def _pre_mixer_kernel(x_ref, g_ref, win_ref, bg_ref, gq_ref, wuq_ref, gkv_ref, wuk_ref, wuvt_ref,
                      ra_ref, rbm_ref, rbp_ref, ckv_all_ref, kr_all_ref,
                      qm_ref, km_ref, vm_ref, om_ref, gate_ref, qa_ref, ka_ref, vt_ref, ckv_ref, kr_ref):
    del ckv_all_ref, kr_all_ref
    xb = _rms(x_ref[...], g_ref[...]).astype(BF16)

    def proj(col, width):
        return jnp.dot(xb, win_ref[:, col:col + width], preferred_element_type=F32)

    qm_ref[...] = proj(COL_QM, M_HEADS * M_DK).astype(BF16)
    km_ref[...] = (proj(COL_KM, M_HEADS * M_DK) * (M_DK ** -0.5)).astype(BF16)
    vm_ref[...] = proj(COL_VM, M_WIDTH).astype(BF16)
    om_ref[...] = proj(COL_OM, M_WIDTH)

    gates = proj(COL_G, LANES) + bg_ref[...]
    lane = lax.broadcasted_iota(jnp.int32, gates.shape, 1)
    log_f = jnp.minimum(gates, 0.0) - jnp.log1p(jnp.exp(-jnp.abs(gates)))
    gate_ref[...] = jnp.where(lane < M_HEADS, gates, log_f)

    ra, rbm, rbp = ra_ref[...], rbm_ref[...], rbp_ref[...]

    cqn = _rms(proj(COL_CQ, Q_RANK), gq_ref[...]).astype(BF16)
    q = jnp.dot(cqn, wuq_ref[...], preferred_element_type=F32)
    for h in range(A_HEADS):
        sl = slice(h * HEAD_SLOT, (h + 1) * HEAD_SLOT)
        qa_ref[:, sl] = (_rope(q[:, sl], ra, rbm, rbp) * (ATTN_SCALE * LOG2E)).astype(BF16)

    ckvn = _rms(proj(COL_CKV, KV_RANK), gkv_ref[...])
    ckv_ref[...] = ckvn
    ckvb = ckvn.astype(BF16)
    kr = _rope(proj(COL_KR, LANES), ra, rbm, rbp)
    kr_ref[...] = kr[:, ROPE_LANE0:ROPE_LANE0 + ROPE_DIM]
    kmat = jnp.dot(ckvb, wuk_ref[...], preferred_element_type=F32)
    for h in range(A_HEADS):
        sl = slice(h * HEAD_SLOT, (h + 1) * HEAD_SLOT)
        ka_ref[:, sl] = (kmat[:, sl] + kr).astype(BF16)
    vt = lax.dot_general(wuvt_ref[...], ckvb, NT_DIMS, preferred_element_type=F32)
    slot_row = lax.broadcasted_iota(jnp.int32, vt.shape, 0) % V_SLOT
    vt = jnp.where(slot_row == V_DIM, 1.0, vt).astype(BF16)
    for s in range(vt_ref.shape[0]):
        vt_ref[s] = vt[:, s * KV_BLOCK:(s + 1) * KV_BLOCK]


def _pre_mixer(x, layer, prm, rope_tabs, tm, n_rope_blocks, ckv_all, kr_all, name):
    T = x.shape[0]
    ra, rbm, rbp = rope_tabs
    row = lambda w: pl.BlockSpec((tm, w), lambda i: (i, 0))
    rope_spec = pl.BlockSpec((tm, LANES), lambda i: (i % n_rope_blocks, 0))
    lyr = lambda shape: _resident((None,) + shape, (layer, 0, 0))
    untouched = pl.BlockSpec(memory_space=pl.ANY)
    slab = lambda w: pl.BlockSpec((None, tm, w), lambda i: (layer, i, 0))
    kvb = tm // KV_BLOCK
    out_specs = [row(M_HEADS * M_DK), row(M_HEADS * M_DK), row(M_WIDTH), row(M_WIDTH), row(LANES),
                 row(QK_WIDTH), row(QK_WIDTH), pl.BlockSpec((kvb, VT_ROWS, KV_BLOCK), lambda i: (i, 0, 0)),
                 slab(KV_RANK), slab(ROPE_DIM)]
    tok = lambda w, d: jax.ShapeDtypeStruct((T, w), d)
    out_shape = [tok(M_HEADS * M_DK, BF16), tok(M_HEADS * M_DK, BF16), tok(M_WIDTH, BF16), tok(M_WIDTH, F32),
                 tok(LANES, F32), tok(QK_WIDTH, BF16), tok(QK_WIDTH, BF16),
                 jax.ShapeDtypeStruct((T // KV_BLOCK, VT_ROWS, KV_BLOCK), BF16),
                 jax.ShapeDtypeStruct(ckv_all.shape, F32), jax.ShapeDtypeStruct(kr_all.shape, F32)]
    in_specs = [row(D_MODEL), lyr((1, D_MODEL)), lyr((D_MODEL, D_IN_PAD)), lyr((1, LANES)),
                lyr((1, Q_RANK)), lyr((Q_RANK, QK_WIDTH)), lyr((1, KV_RANK)),
                lyr((KV_RANK, QK_WIDTH)), lyr((VT_ROWS, KV_RANK)), rope_spec, rope_spec, rope_spec,
                untouched, untouched]
    return pl.pallas_call(
        _pre_mixer_kernel,
        grid=(T // tm,),
        in_specs=in_specs,
        out_specs=out_specs,
        out_shape=out_shape,
        input_output_aliases={len(in_specs) - 2: len(out_specs) - 2, len(in_specs) - 1: len(out_specs) - 1},
        compiler_params=pltpu.CompilerParams(dimension_semantics=("arbitrary",), vmem_limit_bytes=VMEM_LIMIT),
        name=name,
    )(x, prm["g_pre_mix"], prm["w_in"], prm["b_gates"], prm["g_q_norm"], prm["w_uq"], prm["g_kv_norm"],
      prm["w_uk"], prm["w_uvt"], ra, rbm, rbp, ckv_all, kr_all)


PAIRS = M_HEADS // 2
ST_COLS = LANES


def _split_bf16(x, parts):
    out = []
    for _ in range(parts - 1):
        piece = x.astype(BF16)
        out.append(piece)
        x = x - piece.astype(F32)
    out.append(x.astype(BF16))
    return out


def _dot_pieces(lhs, x, parts, lhs_first):
    acc = None
    for piece in _split_bf16(x, parts):
        d = (jnp.dot(lhs, piece, preferred_element_type=F32) if lhs_first
             else jnp.dot(piece, lhs, preferred_element_type=F32))
        acc = d if acc is None else acc + d
    return acc


def _cummax_rows(x, row):
    d = 1
    while d < x.shape[0]:
        x = jnp.where(row >= d, jnp.maximum(x, pltpu.roll(x, d, axis=0)), x)
        d *= 2
    return x


def _mlstm_kernel(q_ref, k_ref, v_ref, g_ref, st0_ref, m0_ref, h_ref, st_ref, m_ref, *, L, n_sub):
    @pl.when(pl.program_id(1) == 0)
    def _():
        st_ref[...] = st0_ref[...]
        m_ref[...] = m0_ref[...]

    row_w = lax.broadcasted_iota(jnp.int32, (L, M_WIDTH), 0)
    row = lax.broadcasted_iota(jnp.int32, (L, LANES), 0)
    lane = lax.broadcasted_iota(jnp.int32, (L, LANES), 1)
    first_head = lane < M_DK
    key = lane % M_DK
    causal = key <= row
    diagonal = key == row
    tri = (lax.broadcasted_iota(jnp.int32, (L, L), 1) <= lax.broadcasted_iota(jnp.int32, (L, L), 0)).astype(BF16)
    expand = (lax.broadcasted_iota(jnp.int32, (LANES, 2 * M_WIDTH), 0)
              == lax.broadcasted_iota(jnp.int32, (LANES, 2 * M_WIDTH), 1) // M_DV).astype(BF16)
    gate_lane = lax.broadcasted_iota(jnp.int32, (L, LANES), 1)
    st_row_first = lax.broadcasted_iota(jnp.int32, (LANES, LANES), 0) < M_DK
    ones_bd = (st_row_first == (lax.broadcasted_iota(jnp.int32, (LANES, LANES), 1) < M_DK)).astype(BF16)
    pairs = range(PAIRS)
    lanes_of = lambda p: slice(p * LANES, (p + 1) * LANES)

    def split_heads(x, fill_first=0, fill_second=0):
        return (jnp.where(first_head, x, jnp.full_like(x, fill_first)),
                jnp.where(first_head, jnp.full_like(x, fill_second), x))

    def stack_heads(x):
        parts = []
        for half in split_heads(x):
            parts.append(half)
            if L < M_DK:
                parts.append(jnp.zeros((M_DK - L, LANES), x.dtype))
        return jnp.concatenate(parts, axis=0)

    def stage_gates(c):
        rows = slice(c * L, (c + 1) * L)
        gates = g_ref[rows, :]
        csum = _dot_pieces(tri, gates, 3, True)
        wide = _dot_pieces(expand, jnp.where(gate_lane < M_HEADS, gates, csum), 2, False)
        b = wide[:, M_WIDTH:]
        a = wide[:, :M_WIDTH] - b
        return dict(rows=rows, b=b, a=a, a_cummax=_cummax_rows(a, row_w))

    def stage_issue(ck):
        rows, b, a = ck["rows"], ck["b"], ck["a"]
        m0 = m_ref[...]
        mx = jnp.maximum(ck["a_cummax"], m0)
        mx_last = mx[L - 1:L, :]
        m_ref[...] = b[L - 1:L, :] + mx_last
        ck.update(mx=mx, w_inter=jnp.exp(m0 - mx), clamp=jnp.exp(-(b + mx)), decay=jnp.exp(m0 - mx_last))
        w_key = jnp.exp(a - mx_last)
        qs = [q_ref[rows, lanes_of(p)] for p in pairs]
        ks = [k_ref[rows, lanes_of(p)] for p in pairs]
        ck["vs"] = vs = [v_ref[rows, lanes_of(p)] for p in pairs]
        ck["scores"] = [lax.dot_general(qs[p], stack_heads(ks[p]), NT_DIMS, preferred_element_type=F32)
                        for p in pairs]
        ck["inter"] = [jnp.dot(jnp.concatenate(split_heads(qs[p]), axis=0), st_ref[p].astype(BF16),
                               preferred_element_type=F32) for p in pairs]
        upd = []
        for p in pairs:
            kw = (ks[p].astype(F32) * w_key[:, lanes_of(p)]).astype(BF16)
            upd.append(lax.dot_general(jnp.concatenate(split_heads(kw), axis=0),
                                       jnp.concatenate(split_heads(vs[p], 1, 1), axis=0), TN_DIMS,
                                       preferred_element_type=F32))
        ck["upd"] = upd

    def stage_state(ck):
        for p in pairs:
            dec = ck["decay"][:, lanes_of(p)]
            dec_rows = jnp.where(st_row_first, dec[:, 0:1], dec[:, M_DK:M_DK + 1])
            st_ref[p] = dec_rows * st_ref[p] + ck["upd"][p]

    def stage_intra(ck):
        intra = []
        for p in pairs:
            a_row = jnp.sum(jnp.where(diagonal, ck["a"][:, lanes_of(p)], 0.0), axis=0, keepdims=True)
            d_m = jnp.where(causal, a_row, -jnp.inf)
            s_mat = (ck["scores"][p] * jnp.exp(d_m - ck["mx"][:, lanes_of(p)])).astype(BF16)
            intra.append(jnp.dot(s_mat, jnp.concatenate([stack_heads(ck["vs"][p]), ones_bd], axis=1),
                                 preferred_element_type=F32))
        ck["intra"] = intra

    def stage_cell(ck):
        cells = []
        for p in pairs:
            first, second = ck["inter"][p][:L], ck["inter"][p][L:]
            q_c = jnp.where(first_head, first, second)
            q_n = pltpu.roll(jnp.where(first_head, second, first), M_DK, axis=1)
            wi = ck["w_inter"][:, lanes_of(p)]
            num = wi * q_c + ck["intra"][p][:, :LANES]
            den = wi * q_n + ck["intra"][p][:, LANES:]
            h_cell = num / jnp.maximum(jnp.abs(den), ck["clamp"][:, lanes_of(p)])
            cells.append((h_cell, _dot_pieces(ones_bd, h_cell * h_cell, 2, False)))
        ck["cells"] = cells

    def stage_out(ck):
        for p in pairs:
            h_cell, sum_sq = ck["cells"][p]
            h_ref[ck["rows"], lanes_of(p)] = h_cell * lax.rsqrt(sum_sq * (1.0 / M_DV) + RMS_EPS)

    chunks = [stage_gates(c) for c in range(n_sub)]
    for t in range(n_sub + 3):
        live = lambda c: 0 <= c < n_sub
        if live(t - 1):
            stage_state(chunks[t - 1])
        if live(t):
            stage_issue(chunks[t])
        if live(t - 1):
            stage_intra(chunks[t - 1])
        if live(t - 2):
            stage_cell(chunks[t - 2])
        if live(t - 3):
            stage_out(chunks[t - 3])


def _mlstm(qm, km, vm, gates, st0, m0, n_streams, n_chunks, L, n_sub, name):
    T = qm.shape[0]
    steps = n_chunks // n_sub
    tok = pl.BlockSpec((L * n_sub, M_WIDTH), lambda b, c: (b * steps + c, 0))
    gate_spec = pl.BlockSpec((L * n_sub, LANES), lambda b, c: (b * steps + c, 0))
    st_spec = pl.BlockSpec((None, PAIRS, LANES, ST_COLS), lambda b, c: (b, 0, 0, 0))
    m_spec = pl.BlockSpec((None, 1, M_WIDTH), lambda b, c: (b, 0, 0))
    return pl.pallas_call(
        functools.partial(_mlstm_kernel, L=L, n_sub=n_sub),
        grid=(n_streams, steps),
        in_specs=[tok, tok, tok, gate_spec, st_spec, m_spec],
        out_specs=[tok, st_spec, m_spec],
        out_shape=[jax.ShapeDtypeStruct((T, M_WIDTH), F32),
                   jax.ShapeDtypeStruct((n_streams, PAIRS, LANES, ST_COLS), F32),
                   jax.ShapeDtypeStruct((n_streams, 1, M_WIDTH), F32)],
        compiler_params=pltpu.CompilerParams(dimension_semantics=("arbitrary", "arbitrary")),
        name=name,
    )(qm, km, vm, gates, st0, m0)


def _pack_state(C, n):
    B = C.shape[0]
    c4 = C.astype(F32).reshape(B, PAIRS, 2, M_DK, M_DV)
    n4 = jnp.broadcast_to(n.astype(F32).reshape(B, PAIRS, 2, M_DK, 1), (B, PAIRS, 2, M_DK, M_DV))
    return jnp.concatenate([jnp.concatenate([c4[:, :, 0], n4[:, :, 0]], axis=-1),
                            jnp.concatenate([n4[:, :, 1], c4[:, :, 1]], axis=-1)], axis=-2)


def _unpack_state(st):
    B = st.shape[0]
    half = lambda hh: slice(hh * M_DK, (hh + 1) * M_DK)
    C = jnp.stack([st[:, :, half(hh), half(hh)] for hh in (0, 1)], axis=2).reshape(B, M_HEADS, M_DK, M_DV)
    n = jnp.stack([st[:, :, half(hh), (1 - hh) * M_DK] for hh in (0, 1)], axis=2).reshape(B, M_HEADS, M_DK)
    return C, n


def _attn_prompt_kernel(q_ref, k_ref, vt_ref, o_ref, acc_ref, m_ref):
    i = pl.program_id(1)
    tq = q_ref.shape[0]
    acc_ref[...] = jnp.zeros_like(acc_ref)
    m_ref[...] = jnp.full_like(m_ref, -jnp.inf)
    key = lax.broadcasted_iota(jnp.int32, (KEY_SUB, tq), 0)
    qry = lax.broadcasted_iota(jnp.int32, (KEY_SUB, tq), 1)
    items = [(h, sub) for h in range(A_HEADS) for sub in range(tq // KEY_SUB)]

    def block(j, masked):
        def scores(h, sub):
            hsl = slice(h * HEAD_SLOT, (h + 1) * HEAD_SLOT)
            rows = pl.ds(pl.multiple_of(j * tq + sub * KEY_SUB, KEY_SUB), KEY_SUB)
            return lax.dot_general(k_ref[rows, hsl], q_ref[:, hsl], NT_DIMS, preferred_element_type=F32)

        s_next = scores(*items[0])
        for n, (h, sub) in enumerate(items):
            s_t = s_next
            if n + 1 < len(items):
                s_next = scores(*items[n + 1])
            if masked:
                s_t = jnp.where((key + sub * KEY_SUB) // CHUNK <= qry // CHUNK, s_t, -jnp.inf)
            m_old = m_ref[h:h + 1, :]
            m_new = jnp.maximum(m_old, jnp.max(s_t, axis=0, keepdims=True))
            m_ref[h:h + 1, :] = m_new
            p_t = jnp.exp2(s_t - m_new).astype(BF16)
            vt = vt_ref[j, h * V_SLOT:(h + 1) * V_SLOT, sub * KEY_SUB:(sub + 1) * KEY_SUB]
            pv = jnp.dot(vt, p_t, preferred_element_type=F32)[:ACC_ROWS]
            acc_ref[h] = jnp.exp2(m_old - m_new) * acc_ref[h] + pv

    def body(j, carry):
        block(j, False)
        return carry

    lax.fori_loop(0, i, body, 0)
    block(i, True)
    outs = []
    for h in range(A_HEADS):
        acc = acc_ref[h]
        outs.append(acc[:V_DIM, :] / acc[V_DIM:V_DIM + 1, :])
    o_ref[...] = jnp.concatenate(outs, axis=0).T.astype(BF16)


def _attn_prompt(qa, ka, vt, n_streams, seq, name):
    T = qa.shape[0]
    tq = KV_BLOCK
    nq = seq // tq
    return pl.pallas_call(
        _attn_prompt_kernel,
        grid=(n_streams, nq),
        in_specs=[pl.BlockSpec((tq, QK_WIDTH), lambda b, i: (b * nq + i, 0)),
                  pl.BlockSpec((seq, QK_WIDTH), lambda b, i: (b, 0)),
                  pl.BlockSpec((nq, VT_ROWS, tq), lambda b, i: (b, 0, 0))],
        out_specs=pl.BlockSpec((tq, A_WIDTH), lambda b, i: (b * nq + i, 0)),
        out_shape=jax.ShapeDtypeStruct((T, A_WIDTH), BF16),
        scratch_shapes=[pltpu.VMEM((A_HEADS, ACC_ROWS, tq), F32), pltpu.VMEM((A_HEADS, tq), F32)],
        compiler_params=pltpu.CompilerParams(dimension_semantics=("arbitrary", "arbitrary"),
                                             vmem_limit_bytes=VMEM_LIMIT),
        name=name,
    )(qa, ka, vt)


def _attn_sample_kernel(q_ref, ckvn_ref, krn_ref, ckvp_ref, krpt_ref, wukt_ref, wuvs_ref, pick_ref, o_ref):
    L = q_ref.shape[0]
    q_heads = [q_ref[:, h * HEAD_SLOT:(h + 1) * HEAD_SLOT] for h in range(A_HEADS)]
    q_slot = jnp.concatenate(q_heads, axis=0)
    q_lat = jnp.concatenate([jnp.dot(q_heads[h], wukt_ref[h], preferred_element_type=F32)
                             for h in range(A_HEADS)], axis=0).astype(BF16)
    q_rope = jnp.dot(q_slot, pick_ref[...], preferred_element_type=F32).astype(BF16)

    ckvb_p = ckvp_ref[...].astype(BF16)
    ckvb_n = ckvn_ref[...].astype(BF16)
    s_p = (lax.dot_general(q_lat, ckvb_p, NT_DIMS, preferred_element_type=F32)
           + jnp.dot(q_rope, krpt_ref[...].astype(BF16), preferred_element_type=F32))
    s_n = (lax.dot_general(q_lat, ckvb_n, NT_DIMS, preferred_element_type=F32)
           + lax.dot_general(q_rope, krn_ref[...].astype(BF16), NT_DIMS, preferred_element_type=F32))
    m = jnp.maximum(jnp.max(s_p, axis=-1, keepdims=True), jnp.max(s_n, axis=-1, keepdims=True))
    p_p = jnp.exp2(s_p - m)
    p_n = jnp.exp2(s_n - m)
    denom = jnp.sum(p_p, axis=-1, keepdims=True) + jnp.sum(p_n, axis=-1, keepdims=True)
    o_lat = (jnp.dot(p_p.astype(BF16), ckvb_p, preferred_element_type=F32)
             + jnp.dot(p_n.astype(BF16), ckvb_n, preferred_element_type=F32)) / denom
    o_lat = o_lat.astype(BF16)
    for pair in range(A_HEADS // 2):
        h0, h1 = 2 * pair, 2 * pair + 1
        out = (jnp.dot(o_lat[h0 * L:(h0 + 1) * L], wuvs_ref[h0], preferred_element_type=F32)
               + jnp.dot(o_lat[h1 * L:(h1 + 1) * L], wuvs_ref[h1], preferred_element_type=F32))
        o_ref[:, pair * LANES:(pair + 1) * LANES] = out.astype(BF16)


def _attn_sample(qa, ckv_new, kr_new, cache_ckv, cache_krope_t, layer, prm, pick, name):
    T = qa.shape[0]
    n_streams, past = cache_ckv.shape[1], cache_ckv.shape[2]
    L = T // n_streams
    tok = lambda w: pl.BlockSpec((L, w), lambda b: (b, 0))
    new = lambda w: pl.BlockSpec((None, L, w), lambda b: (layer, b, 0))
    return pl.pallas_call(
        _attn_sample_kernel,
        grid=(n_streams,),
        in_specs=[tok(QK_WIDTH), new(KV_RANK), new(ROPE_DIM),
                  pl.BlockSpec((None, None, past, KV_RANK), lambda b: (layer, b, 0, 0)),
                  pl.BlockSpec((None, None, ROPE_DIM, past), lambda b: (layer, b, 0, 0)),
                  _resident((None, A_HEADS, HEAD_SLOT, KV_RANK), (layer, 0, 0, 0)),
                  _resident((None, A_HEADS, KV_RANK, LANES), (layer, 0, 0, 0)),
                  _resident((HEAD_SLOT, ROPE_DIM), (0, 0))],
        out_specs=tok(A_WIDTH),
        out_shape=jax.ShapeDtypeStruct((T, A_WIDTH), BF16),
        compiler_params=pltpu.CompilerParams(dimension_semantics=("arbitrary",), vmem_limit_bytes=VMEM_LIMIT),
        name=name,
    )(qa, ckv_new, kr_new, cache_ckv, cache_krope_t, prm["w_ukt"], prm["w_uvs"], pick)


def _post_mixer_kernel(x_ref, hn_ref, om_ref, at_ref, gh_ref, wo_ref, gpm_ref, gpf_ref, w1_ref, w2_ref, gpo_ref,
                       o_ref):
    tm = x_ref.shape[0]
    groups = [slice(g * tm // ROW_GROUPS, (g + 1) * tm // ROW_GROUPS) for g in range(ROW_GROUPS)]
    mix = []
    for r in groups:
        h_m = jax.nn.sigmoid(om_ref[r, :]) * (hn_ref[r, :] * gh_ref[...])
        mix.append(jnp.dot(jnp.concatenate([h_m.astype(BF16), at_ref[r, :]], axis=-1), wo_ref[...],
                           preferred_element_type=F32))
    x1 = [x_ref[r, :] + _rms(m, gpm_ref[...]) for r, m in zip(groups, mix)]
    hb = [_rms(x, gpf_ref[...]).astype(BF16) for x in x1]
    acc = [None] * ROW_GROUPS
    for lo, hi in FF_CHUNKS:
        gate_up = [(jnp.dot(h, w1_ref[:, lo:hi], preferred_element_type=F32),
                    jnp.dot(h, w1_ref[:, D_FF + lo:D_FF + hi], preferred_element_type=F32)) for h in hb]
        for g, (gate, up) in enumerate(gate_up):
            act = (gate * jax.nn.sigmoid(gate) * up).astype(BF16)
            part = jnp.dot(act, w2_ref[lo:hi, :], preferred_element_type=F32)
            acc[g] = part if acc[g] is None else acc[g] + part
    for r, x, a in zip(groups, x1, acc):
        o_ref[r, :] = x + _rms(a, gpo_ref[...])


def _post_mixer(x, hn, om, attn, layer, prm, tm, name):
    T = x.shape[0]
    row = lambda w: pl.BlockSpec((tm, w), lambda i: (i, 0))
    lyr = lambda shape: _resident((None,) + shape, (layer, 0, 0))
    return pl.pallas_call(
        _post_mixer_kernel,
        grid=(T // tm,),
        in_specs=[row(D_MODEL), row(M_WIDTH), row(M_WIDTH), row(A_WIDTH), lyr((1, M_WIDTH)),
                  lyr((D_MODEL, D_MODEL)), lyr((1, D_MODEL)), lyr((1, D_MODEL)), lyr((D_MODEL, 2 * D_FF)),
                  lyr((D_FF, D_MODEL)), lyr((1, D_MODEL))],
        out_specs=row(D_MODEL),
        out_shape=jax.ShapeDtypeStruct((T, D_MODEL), F32),
        compiler_params=pltpu.CompilerParams(dimension_semantics=("arbitrary",), vmem_limit_bytes=VMEM_LIMIT),
        name=name,
    )(x, hn, om, attn, prm["g_mlstm_head"], prm["w_out"], prm["g_post_mix"], prm["g_pre_ffn"], prm["w_ffn_in"],
      prm["w_ffn_out"], prm["g_post_ffn"])


def _prepare_params(g_pre_mix, g_post_mix, g_pre_ffn, g_post_ffn, w_in, b_gates, g_mlstm_head, g_q_norm, w_uq,
                    g_kv_norm, w_ukv, w_out, w_ffn_in, w_ffn_out):
    depth = w_in.shape[0]
    sizes = (M_HEADS * M_DK, M_HEADS * M_DK, M_WIDTH, M_WIDTH, 2 * M_HEADS, Q_RANK, KV_RANK, ROPE_DIM)
    offs = [0]
    for s in sizes:
        offs.append(offs[-1] + s)
    part = lambda i: w_in[:, :, offs[i]:offs[i + 1]]
    zeros = lambda w: jnp.zeros((depth, D_MODEL, w), w_in.dtype)
    w_in_r = jnp.concatenate(
        [part(0), part(1), part(2), part(3), part(5), part(6),
         zeros(ROPE_LANE0), part(7), zeros(LANES - ROPE_LANE0 - ROPE_DIM),
         part(4), zeros(LANES - 2 * M_HEADS)], axis=-1).astype(BF16)
    w_uq_r = jnp.pad(w_uq.reshape(depth, Q_RANK, A_HEADS, NOPE_DIM + ROPE_DIM),
                     ((0, 0), (0, 0), (0, 0), (0, HEAD_SLOT - NOPE_DIM - ROPE_DIM)))
    w_uq_r = w_uq_r.reshape(depth, Q_RANK, QK_WIDTH).astype(BF16)
    w_ukv4 = w_ukv.reshape(depth, KV_RANK, A_HEADS, NOPE_DIM + V_DIM)
    w_uk4 = jnp.pad(w_ukv4[..., :NOPE_DIM], ((0, 0), (0, 0), (0, 0), (0, HEAD_SLOT - NOPE_DIM)))
    w_uk_r = w_uk4.reshape(depth, KV_RANK, QK_WIDTH).astype(BF16)
    w_ukt = w_uk4.transpose(0, 2, 3, 1).astype(BF16)
    w_uv4 = w_ukv4[..., NOPE_DIM:]
    w_uvt = jnp.pad(w_uv4, ((0, 0), (0, 0), (0, 0), (0, V_SLOT - V_DIM)))
    w_uvt = w_uvt.reshape(depth, KV_RANK, VT_ROWS).transpose(0, 2, 1).astype(BF16)
    w_uvs = jnp.stack([jnp.pad(w_uv4[:, :, h, :], ((0, 0), (0, 0), ((h % 2) * V_DIM, (1 - h % 2) * V_DIM)))
                       for h in range(A_HEADS)], axis=1).astype(BF16)
    row = lambda g: g.astype(F32)[:, None, :]
    return {
        "g_pre_mix": row(g_pre_mix), "g_post_mix": row(g_post_mix), "g_pre_ffn": row(g_pre_ffn),
        "g_post_ffn": row(g_post_ffn), "g_mlstm_head": row(g_mlstm_head), "g_q_norm": row(g_q_norm),
        "g_kv_norm": row(g_kv_norm),
        "b_gates": jnp.pad(b_gates.astype(F32), ((0, 0), (0, LANES - 2 * M_HEADS)))[:, None, :],
        "w_in": w_in_r, "w_uq": w_uq_r, "w_uk": w_uk_r, "w_ukt": w_ukt, "w_uvt": w_uvt, "w_uvs": w_uvs,
        "w_out": w_out.astype(BF16), "w_ffn_in": w_ffn_in.astype(BF16), "w_ffn_out": w_ffn_out.astype(BF16),
    }


def _rope_tables(pos):
    inv = ROPE_THETA ** (-jnp.arange(ROPE_HALF, dtype=F32) / ROPE_HALF)
    ang = pos.astype(F32)[:, None] * inv[None, :]
    cos, sin = jnp.cos(ang), jnp.sin(ang)
    n = pos.shape[0]
    z = lambda w: jnp.zeros((n, w), F32)
    tail = LANES - ROPE_LANE0 - ROPE_DIM
    ra = jnp.concatenate([jnp.ones((n, ROPE_LANE0), F32), cos, cos, z(tail)], axis=-1)
    rbm = jnp.concatenate([z(ROPE_LANE0), -sin, z(ROPE_HALF), z(tail)], axis=-1)
    rbp = jnp.concatenate([z(ROPE_LANE0), z(ROPE_HALF), sin, z(tail)], axis=-1)
    return ra, rbm, rbp


def _rope_lane_picker():
    r = lax.broadcasted_iota(jnp.int32, (HEAD_SLOT, ROPE_DIM), 0)
    c = lax.broadcasted_iota(jnp.int32, (HEAD_SLOT, ROPE_DIM), 1)
    return (r == (c + ROPE_LANE0)).astype(BF16)


TM_PROMPT = 512
TM_SAMPLE = 256
MLSTM_SUB = 8
ROW_GROUPS = 2


def kernel(x_prompt, x_sample, cache_ckv, cache_krope, state_mlstm_C, state_mlstm_n, state_mlstm_m,
           g_pre_mix, g_post_mix, g_pre_ffn, g_post_ffn, w_in, b_gates, g_mlstm_head,
           g_q_norm, w_uq, g_kv_norm, w_ukv, w_out, w_ffn_in, w_ffn_out):
    B, S, _ = x_prompt.shape
    Bs, Ls, _ = x_sample.shape
    depth, _, past, _ = cache_ckv.shape
    assert S % TM_PROMPT == 0 and TM_PROMPT % KV_BLOCK == 0 and KV_BLOCK % CHUNK == 0
    assert TM_SAMPLE % Ls == 0 and TM_SAMPLE % KV_BLOCK == 0 and (Bs * Ls) % TM_SAMPLE == 0

    prm = _prepare_params(g_pre_mix, g_post_mix, g_pre_ffn, g_post_ffn, w_in, b_gates, g_mlstm_head, g_q_norm,
                          w_uq, g_kv_norm, w_ukv, w_out, w_ffn_in, w_ffn_out)
    rope_p = _rope_tables(jnp.arange(S))
    rope_s = _rope_tables(past + jnp.arange(TM_SAMPLE) % Ls)
    pick = _rope_lane_picker()
    cache_krope_t = jnp.swapaxes(cache_krope, 2, 3)

    xp = x_prompt.reshape(B * S, D_MODEL)
    xs = x_sample.reshape(Bs * Ls, D_MODEL)
    zero_st = jnp.zeros((B, PAIRS, LANES, ST_COLS), F32)
    zero_m = jnp.zeros((B, 1, M_WIDTH), F32)
    m_in = jnp.repeat(state_mlstm_m.astype(F32), M_DV, axis=-1)[:, :, None, :]

    ckv_p, kr_p = jnp.zeros((depth, B * S, KV_RANK), F32), jnp.zeros((depth, B * S, ROPE_DIM), F32)
    ckv_s, kr_s = jnp.zeros((depth, Bs * Ls, KV_RANK), F32), jnp.zeros((depth, Bs * Ls, ROPE_DIM), F32)

    outs_p = {k: [] for k in ("C", "n", "m")}
    outs_s = {k: [] for k in ("C", "n", "m")}
    for l in range(depth):
        qm, km, vm, om, gates, qa, ka, vt, ckv_p, kr_p = _pre_mixer(
            xp, l, prm, rope_p, TM_PROMPT, S // TM_PROMPT, ckv_p, kr_p, f"pre_mixer_p{l}")
        hn, st, m = _mlstm(qm, km, vm, gates, zero_st, zero_m, B, S // CHUNK, CHUNK, MLSTM_SUB, f"mlstm_p{l}")
        attn = _attn_prompt(qa, ka, vt, B, S, f"attn_p{l}")
        xp = _post_mixer(xp, hn, om, attn, l, prm, TM_PROMPT, f"post_mixer_p{l}")
        for key, val in zip(("C", "n", "m"), _unpack_state(st) + (m,)):
            outs_p[key].append(val)
        qm, km, vm, om, gates, qa, _, _, ckv_s, kr_s = _pre_mixer(
            xs, l, prm, rope_s, TM_SAMPLE, 1, ckv_s, kr_s, f"pre_mixer_s{l}")
        hn, st, m = _mlstm(qm, km, vm, gates, _pack_state(state_mlstm_C[l], state_mlstm_n[l]), m_in[l],
                           Bs, 1, Ls, 1, f"mlstm_s{l}")
        attn = _attn_sample(qa, ckv_s, kr_s, cache_ckv, cache_krope_t, l, prm, pick, f"attn_s{l}")
        xs = _post_mixer(xs, hn, om, attn, l, prm, TM_SAMPLE, f"post_mixer_s{l}")
        for key, val in zip(("C", "n", "m"), _unpack_state(st) + (m,)):
            outs_s[key].append(val)

    def collect(ckv, kr, o, nb, ln):
        return (ckv.reshape(depth, nb, ln, KV_RANK), kr.reshape(depth, nb, ln, ROPE_DIM),
                jnp.stack(o["C"]), jnp.stack(o["n"]), jnp.stack(o["m"])[:, :, 0, ::M_DV])

    return ((xp.reshape(B, S, D_MODEL), xs.reshape(Bs, Ls, D_MODEL))
            + collect(ckv_p, kr_p, outs_p, B, S) + collect(ckv_s, kr_s, outs_s, Bs, Ls))
```

```python
import functools

import jax
import jax.numpy as jnp
from jax import lax
from jax.experimental import pallas as pl
from jax.experimental.pallas import tpu as pltpu

F32 = jnp.float32
BF16 = jnp.bfloat16

D_MODEL = 1024
DEPTH = 4
CHUNK = 64
M_HEADS = 8
M_DK = 64
M_DV = 64
M_WIDTH = M_HEADS * M_DV
A_HEADS = 8
NOPE_DIM = 64
ROPE_DIM = 32
ROPE_HALF = ROPE_DIM // 2
V_DIM = 64
A_WIDTH = A_HEADS * V_DIM
Q_RANK = 256
KV_RANK = 256
ROPE_THETA = 10000.0
D_FF = 2816
RMS_EPS = 1e-6
ATTN_SCALE = (NOPE_DIM + ROPE_DIM) ** -0.5

LANES = 128
HEAD_SLOT = LANES
QK_WIDTH = A_HEADS * HEAD_SLOT
V_SLOT = LANES
VT_ROWS = A_HEADS * V_SLOT
ACC_ROWS = V_DIM + 16
LOG2E = 1.4426950408889634
ROPE_LANE0 = NOPE_DIM

COL_QM = 0
COL_KM = COL_QM + M_HEADS * M_DK
COL_VM = COL_KM + M_HEADS * M_DK
COL_OM = COL_VM + M_WIDTH
COL_CQ = COL_OM + M_WIDTH
COL_CKV = COL_CQ + Q_RANK
COL_KR = COL_CKV + KV_RANK
COL_G = COL_KR + LANES
D_IN_PAD = COL_G + LANES

FF_CHUNKS = ((0, 768), (768, 1792), (1792, 2816))
KV_BLOCK = 256
KEY_SUB = 128

VMEM_LIMIT = 56 * 1024 * 1024

NT_DIMS = (((1,), (1,)), ((), ()))
TN_DIMS = (((0,), (0,)), ((), ()))


def _resident(block_shape, index):
    return pl.BlockSpec(block_shape, lambda *_: index, pipeline_mode=pl.Buffered(1))


def _rms(x, g):
    return x * lax.rsqrt(jnp.mean(x * x, axis=-1, keepdims=True) + RMS_EPS) * g


def _rope(x, ra, rbm, rbp):
    return x * ra + pltpu.roll(x, LANES - ROPE_HALF, axis=1) * rbm + pltpu.roll(x, ROPE_HALF, axis=1) * rbp


def _pre_mixer_kernel(x_ref, g_ref, win_ref, bg_ref, gq_ref, wuq_ref, gkv_ref, wuk_ref, wuvt_ref,
                      ra_ref, rbm_ref, rbp_ref, ckv_all_ref, kr_all_ref,
                      qm_ref, km_ref, vm_ref, om_ref, gate_ref, qa_ref, ka_ref, vt_ref, ckv_ref, kr_ref):
    del ckv_all_ref, kr_all_ref
    xb = _rms(x_ref[...], g_ref[...]).astype(BF16)

    def proj(col, width):
        return jnp.dot(xb, win_ref[:, col:col + width], preferred_element_type=F32)

    qm_ref[...] = proj(COL_QM, M_HEADS * M_DK).astype(BF16)
    km_ref[...] = (proj(COL_KM, M_HEADS * M_DK) * (M_DK ** -0.5)).astype(BF16)
    vm_ref[...] = proj(COL_VM, M_WIDTH).astype(BF16)
    om_ref[...] = proj(COL_OM, M_WIDTH)

    gates = proj(COL_G, LANES) + bg_ref[...]
    lane = lax.broadcasted_iota(jnp.int32, gates.shape, 1)
    log_f = jnp.minimum(gates, 0.0) - jnp.log1p(jnp.exp(-jnp.abs(gates)))
    gate_ref[...] = jnp.where(lane < M_HEADS, gates, log_f)

    ra, rbm, rbp = ra_ref[...], rbm_ref[...], rbp_ref[...]

    cqn = _rms(proj(COL_CQ, Q_RANK), gq_ref[...]).astype(BF16)
    q = jnp.dot(cqn, wuq_ref[...], preferred_element_type=F32)
    for h in range(A_HEADS):
        sl = slice(h * HEAD_SLOT, (h + 1) * HEAD_SLOT)
        qa_ref[:, sl] = (_rope(q[:, sl], ra, rbm, rbp) * (ATTN_SCALE * LOG2E)).astype(BF16)

    ckvn = _rms(proj(COL_CKV, KV_RANK), gkv_ref[...])
    ckv_ref[...] = ckvn
    ckvb = ckvn.astype(BF16)
    kr = _rope(proj(COL_KR, LANES), ra, rbm, rbp)
    kr_ref[...] = kr[:, ROPE_LANE0:ROPE_LANE0 + ROPE_DIM]
    kmat = jnp.dot(ckvb, wuk_ref[...], preferred_element_type=F32)
    for h in range(A_HEADS):
        sl = slice(h * HEAD_SLOT, (h + 1) * HEAD_SLOT)
        ka_ref[:, sl] = (kmat[:, sl] + kr).astype(BF16)
    vt = lax.dot_general(wuvt_ref[...], ckvb, NT_DIMS, preferred_element_type=F32)
    slot_row = lax.broadcasted_iota(jnp.int32, vt.shape, 0) % V_SLOT
    vt = jnp.where(slot_row == V_DIM, 1.0, vt).astype(BF16)
    for s in range(vt_ref.shape[0]):
        vt_ref[s] = vt[:, s * KV_BLOCK:(s + 1) * KV_BLOCK]


def _pre_mixer(x, layer, prm, rope_tabs, tm, n_rope_blocks, ckv_all, kr_all, name):
    T = x.shape[0]
    ra, rbm, rbp = rope_tabs
    row = lambda w: pl.BlockSpec((tm, w), lambda i: (i, 0))
    rope_spec = pl.BlockSpec((tm, LANES), lambda i: (i % n_rope_blocks, 0))
    lyr = lambda shape: _resident((None,) + shape, (layer, 0, 0))
    untouched = pl.BlockSpec(memory_space=pl.ANY)
    slab = lambda w: pl.BlockSpec((None, tm, w), lambda i: (layer, i, 0))
    kvb = tm // KV_BLOCK
    out_specs = [row(M_HEADS * M_DK), row(M_HEADS * M_DK), row(M_WIDTH), row(M_WIDTH), row(LANES),
                 row(QK_WIDTH), row(QK_WIDTH), pl.BlockSpec((kvb, VT_ROWS, KV_BLOCK), lambda i: (i, 0, 0)),
                 slab(KV_RANK), slab(ROPE_DIM)]
    tok = lambda w, d: jax.ShapeDtypeStruct((T, w), d)
    out_shape = [tok(M_HEADS * M_DK, BF16), tok(M_HEADS * M_DK, BF16), tok(M_WIDTH, BF16), tok(M_WIDTH, F32),
                 tok(LANES, F32), tok(QK_WIDTH, BF16), tok(QK_WIDTH, BF16),
                 jax.ShapeDtypeStruct((T // KV_BLOCK, VT_ROWS, KV_BLOCK), BF16),
                 jax.ShapeDtypeStruct(ckv_all.shape, F32), jax.ShapeDtypeStruct(kr_all.shape, F32)]
    in_specs = [row(D_MODEL), lyr((1, D_MODEL)), lyr((D_MODEL, D_IN_PAD)), lyr((1, LANES)),
                lyr((1, Q_RANK)), lyr((Q_RANK, QK_WIDTH)), lyr((1, KV_RANK)),
                lyr((KV_RANK, QK_WIDTH)), lyr((VT_ROWS, KV_RANK)), rope_spec, rope_spec, rope_spec,
                untouched, untouched]
    return pl.pallas_call(
        _pre_mixer_kernel,
        grid=(T // tm,),
        in_specs=in_specs,
        out_specs=out_specs,
        out_shape=out_shape,
        input_output_aliases={len(in_specs) - 2: len(out_specs) - 2, len(in_specs) - 1: len(out_specs) - 1},
        compiler_params=pltpu.CompilerParams(dimension_semantics=("arbitrary",), vmem_limit_bytes=VMEM_LIMIT),
        name=name,
    )(x, prm["g_pre_mix"], prm["w_in"], prm["b_gates"], prm["g_q_norm"], prm["w_uq"], prm["g_kv_norm"],
      prm["w_uk"], prm["w_uvt"], ra, rbm, rbp, ckv_all, kr_all)


PAIRS = M_HEADS // 2
ST_COLS = LANES


def _split_bf16(x, parts):
    out = []
    for _ in range(parts - 1):
        piece = x.astype(BF16)
        out.append(piece)
        x = x - piece.astype(F32)
    out.append(x.astype(BF16))
    return out


def _dot_pieces(lhs, x, parts, lhs_first):
    acc = None
    for piece in _split_bf16(x, parts):
        d = (jnp.dot(lhs, piece, preferred_element_type=F32) if lhs_first
             else jnp.dot(piece, lhs, preferred_element_type=F32))
        acc = d if acc is None else acc + d
    return acc


def _cummax_rows(x, row):
    d = 1
    while d < x.shape[0]:
        x = jnp.where(row >= d, jnp.maximum(x, pltpu.roll(x, d, axis=0)), x)
        d *= 2
    return x


def _mlstm_kernel(q_ref, k_ref, v_ref, g_ref, st0_ref, m0_ref, h_ref, st_ref, m_ref, *, L, n_sub):
    @pl.when(pl.program_id(1) == 0)
    def _():
        st_ref[...] = st0_ref[...]
        m_ref[...] = m0_ref[...]

    row_w = lax.broadcasted_iota(jnp.int32, (L, M_WIDTH), 0)
    row = lax.broadcasted_iota(jnp.int32, (L, LANES), 0)
    lane = lax.broadcasted_iota(jnp.int32, (L, LANES), 1)
    first_head = lane < M_DK
    key = lane % M_DK
    causal = key <= row
    diagonal = key == row
    tri = (lax.broadcasted_iota(jnp.int32, (L, L), 1) <= lax.broadcasted_iota(jnp.int32, (L, L), 0)).astype(BF16)
    expand = (lax.broadcasted_iota(jnp.int32, (LANES, 2 * M_WIDTH), 0)
              == lax.broadcasted_iota(jnp.int32, (LANES, 2 * M_WIDTH), 1) // M_DV).astype(BF16)
    gate_lane = lax.broadcasted_iota(jnp.int32, (L, LANES), 1)
    st_row_first = lax.broadcasted_iota(jnp.int32, (LANES, LANES), 0) < M_DK
    ones_bd = (st_row_first == (lax.broadcasted_iota(jnp.int32, (LANES, LANES), 1) < M_DK)).astype(BF16)
    pairs = range(PAIRS)
    lanes_of = lambda p: slice(p * LANES, (p + 1) * LANES)

    def split_heads(x, fill_first=0, fill_second=0):
        return (jnp.where(first_head, x, jnp.full_like(x, fill_first)),
                jnp.where(first_head, jnp.full_like(x, fill_second), x))

    def stack_heads(x):
        parts = []
        for half in split_heads(x):
            parts.append(half)
            if L < M_DK:
                parts.append(jnp.zeros((M_DK - L, LANES), x.dtype))
        return jnp.concatenate(parts, axis=0)

    def stage_gates(c):
        rows = slice(c * L, (c + 1) * L)
        gates = g_ref[rows, :]
        csum = _dot_pieces(tri, gates, 3, True)
        wide = _dot_pieces(expand, jnp.where(gate_lane < M_HEADS, gates, csum), 2, False)
        b = wide[:, M_WIDTH:]
        a = wide[:, :M_WIDTH] - b
        return dict(rows=rows, b=b, a=a, a_cummax=_cummax_rows(a, row_w))

    def stage_issue(ck):
        rows, b, a = ck["rows"], ck["b"], ck["a"]
        m0 = m_ref[...]
        mx = jnp.maximum(ck["a_cummax"], m0)
        mx_last = mx[L - 1:L, :]
        m_ref[...] = b[L - 1:L, :] + mx_last
        ck.update(mx=mx, w_inter=jnp.exp(m0 - mx), clamp=jnp.exp(-(b + mx)), decay=jnp.exp(m0 - mx_last))
        w_key = jnp.exp(a - mx_last)
        qs = [q_ref[rows, lanes_of(p)] for p in pairs]
        ks = [k_ref[rows, lanes_of(p)] for p in pairs]
        ck["vs"] = vs = [v_ref[rows, lanes_of(p)] for p in pairs]
        ck["scores"] = [lax.dot_general(qs[p], stack_heads(ks[p]), NT_DIMS, preferred_element_type=F32)
                        for p in pairs]
        ck["inter"] = [jnp.dot(jnp.concatenate(split_heads(qs[p]), axis=0), st_ref[p].astype(BF16),
                               preferred_element_type=F32) for p in pairs]
        upd = []
        for p in pairs:
            kw = (ks[p].astype(F32) * w_key[:, lanes_of(p)]).astype(BF16)
            upd.append(lax.dot_general(jnp.concatenate(split_heads(kw), axis=0),
                                       jnp.concatenate(split_heads(vs[p], 1, 1), axis=0), TN_DIMS,
                                       preferred_element_type=F32))
        ck["upd"] = upd

    def stage_state(ck):
        for p in pairs:
            dec = ck["decay"][:, lanes_of(p)]
            dec_rows = jnp.where(st_row_first, dec[:, 0:1], dec[:, M_DK:M_DK + 1])
            st_ref[p] = dec_rows * st_ref[p] + ck["upd"][p]

    def stage_intra(ck):
        intra = []
        for p in pairs:
            a_row = jnp.sum(jnp.where(diagonal, ck["a"][:, lanes_of(p)], 0.0), axis=0, keepdims=True)
            d_m = jnp.where(causal, a_row, -jnp.inf)
            s_mat = (ck["scores"][p] * jnp.exp(d_m - ck["mx"][:, lanes_of(p)])).astype(BF16)
            intra.append(jnp.dot(s_mat, jnp.concatenate([stack_heads(ck["vs"][p]), ones_bd], axis=1),
                                 preferred_element_type=F32))
        ck["intra"] = intra

    def stage_cell(ck):
        cells = []
        for p in pairs:
            first, second = ck["inter"][p][:L], ck["inter"][p][L:]
            q_c = jnp.where(first_head, first, second)
            q_n = pltpu.roll(jnp.where(first_head, second, first), M_DK, axis=1)
            wi = ck["w_inter"][:, lanes_of(p)]
            num = wi * q_c + ck["intra"][p][:, :LANES]
            den = wi * q_n + ck["intra"][p][:, LANES:]
            h_cell = num / jnp.maximum(jnp.abs(den), ck["clamp"][:, lanes_of(p)])
            cells.append((h_cell, _dot_pieces(ones_bd, h_cell * h_cell, 2, False)))
        ck["cells"] = cells

    def stage_out(ck):
        for p in pairs:
            h_cell, sum_sq = ck["cells"][p]
            h_ref[ck["rows"], lanes_of(p)] = h_cell * lax.rsqrt(sum_sq * (1.0 / M_DV) + RMS_EPS)

    chunks = [stage_gates(c) for c in range(n_sub)]
    for t in range(n_sub + 3):
        live = lambda c: 0 <= c < n_sub
        if live(t - 1):
            stage_state(chunks[t - 1])
        if live(t):
            stage_issue(chunks[t])
        if live(t - 1):
            stage_intra(chunks[t - 1])
        if live(t - 2):
            stage_cell(chunks[t - 2])
        if live(t - 3):
            stage_out(chunks[t - 3])


def _mlstm(qm, km, vm, gates, st0, m0, n_streams, n_chunks, L, n_sub, name):
    T = qm.shape[0]
    steps = n_chunks // n_sub
    tok = pl.BlockSpec((L * n_sub, M_WIDTH), lambda b, c: (b * steps + c, 0))
    gate_spec = pl.BlockSpec((L * n_sub, LANES), lambda b, c: (b * steps + c, 0))
    st_spec = pl.BlockSpec((None, PAIRS, LANES, ST_COLS), lambda b, c: (b, 0, 0, 0))
    m_spec = pl.BlockSpec((None, 1, M_WIDTH), lambda b, c: (b, 0, 0))
    return pl.pallas_call(
        functools.partial(_mlstm_kernel, L=L, n_sub=n_sub),
        grid=(n_streams, steps),
        in_specs=[tok, tok, tok, gate_spec, st_spec, m_spec],
        out_specs=[tok, st_spec, m_spec],
        out_shape=[jax.ShapeDtypeStruct((T, M_WIDTH), F32),
                   jax.ShapeDtypeStruct((n_streams, PAIRS, LANES, ST_COLS), F32),
                   jax.ShapeDtypeStruct((n_streams, 1, M_WIDTH), F32)],
        compiler_params=pltpu.CompilerParams(dimension_semantics=("arbitrary", "arbitrary")),
        name=name,
    )(qm, km, vm, gates, st0, m0)


def _pack_state(C, n):
    B = C.shape[0]
    c4 = C.astype(F32).reshape(B, PAIRS, 2, M_DK, M_DV)
    n4 = jnp.broadcast_to(n.astype(F32).reshape(B, PAIRS, 2, M_DK, 1), (B, PAIRS, 2, M_DK, M_DV))
    return jnp.concatenate([jnp.concatenate([c4[:, :, 0], n4[:, :, 0]], axis=-1),
                            jnp.concatenate([n4[:, :, 1], c4[:, :, 1]], axis=-1)], axis=-2)


def _unpack_state(st):
    B = st.shape[0]
    half = lambda hh: slice(hh * M_DK, (hh + 1) * M_DK)
    C = jnp.stack([st[:, :, half(hh), half(hh)] for hh in (0, 1)], axis=2).reshape(B, M_HEADS, M_DK, M_DV)
    n = jnp.stack([st[:, :, half(hh), (1 - hh) * M_DK] for hh in (0, 1)], axis=2).reshape(B, M_HEADS, M_DK)
    return C, n


def _attn_prompt_kernel(q_ref, k_ref, vt_ref, o_ref, acc_ref, m_ref):
    i = pl.program_id(1)
    tq = q_ref.shape[0]
    acc_ref[...] = jnp.zeros_like(acc_ref)
    m_ref[...] = jnp.full_like(m_ref, -jnp.inf)
    key = lax.broadcasted_iota(jnp.int32, (KEY_SUB, tq), 0)
    qry = lax.broadcasted_iota(jnp.int32, (KEY_SUB, tq), 1)
    items = [(h, sub) for h in range(A_HEADS) for sub in range(tq // KEY_SUB)]

    def block(j, masked):
        def scores(h, sub):
            hsl = slice(h * HEAD_SLOT, (h + 1) * HEAD_SLOT)
            rows = pl.ds(pl.multiple_of(j * tq + sub * KEY_SUB, KEY_SUB), KEY_SUB)
            return lax.dot_general(k_ref[rows, hsl], q_ref[:, hsl], NT_DIMS, preferred_element_type=F32)

        s_next = scores(*items[0])
        for n, (h, sub) in enumerate(items):
            s_t = s_next
            if n + 1 < len(items):
                s_next = scores(*items[n + 1])
            if masked:
                s_t = jnp.where((key + sub * KEY_SUB) // CHUNK <= qry // CHUNK, s_t, -jnp.inf)
            m_old = m_ref[h:h + 1, :]
            m_new = jnp.maximum(m_old, jnp.max(s_t, axis=0, keepdims=True))
            m_ref[h:h + 1, :] = m_new
            p_t = jnp.exp2(s_t - m_new).astype(BF16)
            vt = vt_ref[j, h * V_SLOT:(h + 1) * V_SLOT, sub * KEY_SUB:(sub + 1) * KEY_SUB]
            pv = jnp.dot(vt, p_t, preferred_element_type=F32)[:ACC_ROWS]
            acc_ref[h] = jnp.exp2(m_old - m_new) * acc_ref[h] + pv

    def body(j, carry):
        block(j, False)
        return carry

    lax.fori_loop(0, i, body, 0)
    block(i, True)
    outs = []
    for h in range(A_HEADS):
        acc = acc_ref[h]
        outs.append(acc[:V_DIM, :] / acc[V_DIM:V_DIM + 1, :])
    o_ref[...] = jnp.concatenate(outs, axis=0).T.astype(BF16)


def _attn_prompt(qa, ka, vt, n_streams, seq, name):
    T = qa.shape[0]
    tq = KV_BLOCK
    nq = seq // tq
    return pl.pallas_call(
        _attn_prompt_kernel,
        grid=(n_streams, nq),
        in_specs=[pl.BlockSpec((tq, QK_WIDTH), lambda b, i: (b * nq + i, 0)),
                  pl.BlockSpec((seq, QK_WIDTH), lambda b, i: (b, 0)),
                  pl.BlockSpec((nq, VT_ROWS, tq), lambda b, i: (b, 0, 0))],
        out_specs=pl.BlockSpec((tq, A_WIDTH), lambda b, i: (b * nq + i, 0)),
        out_shape=jax.ShapeDtypeStruct((T, A_WIDTH), BF16),
        scratch_shapes=[pltpu.VMEM((A_HEADS, ACC_ROWS, tq), F32), pltpu.VMEM((A_HEADS, tq), F32)],
        compiler_params=pltpu.CompilerParams(dimension_semantics=("arbitrary", "arbitrary"),
                                             vmem_limit_bytes=VMEM_LIMIT),
        name=name,
    )(qa, ka, vt)


def _attn_sample_kernel(q_ref, ckvn_ref, krn_ref, ckvp_ref, krpt_ref, wukt_ref, wuvs_ref, pick_ref, o_ref):
    L = q_ref.shape[0]
    q_heads = [q_ref[:, h * HEAD_SLOT:(h + 1) * HEAD_SLOT] for h in range(A_HEADS)]
    q_slot = jnp.concatenate(q_heads, axis=0)
    q_lat = jnp.concatenate([jnp.dot(q_heads[h], wukt_ref[h], preferred_element_type=F32)
                             for h in range(A_HEADS)], axis=0).astype(BF16)
    q_rope = jnp.dot(q_slot, pick_ref[...], preferred_element_type=F32).astype(BF16)

    ckvb_p = ckvp_ref[...].astype(BF16)
    ckvb_n = ckvn_ref[...].astype(BF16)
    s_p = (lax.dot_general(q_lat, ckvb_p, NT_DIMS, preferred_element_type=F32)
           + jnp.dot(q_rope, krpt_ref[...].astype(BF16), preferred_element_type=F32))
    s_n = (lax.dot_general(q_lat, ckvb_n, NT_DIMS, preferred_element_type=F32)
           + lax.dot_general(q_rope, krn_ref[...].astype(BF16), NT_DIMS, preferred_element_type=F32))
    m = jnp.maximum(jnp.max(s_p, axis=-1, keepdims=True), jnp.max(s_n, axis=-1, keepdims=True))
    p_p = jnp.exp2(s_p - m)
    p_n = jnp.exp2(s_n - m)
    denom = jnp.sum(p_p, axis=-1, keepdims=True) + jnp.sum(p_n, axis=-1, keepdims=True)
    o_lat = (jnp.dot(p_p.astype(BF16), ckvb_p, preferred_element_type=F32)
             + jnp.dot(p_n.astype(BF16), ckvb_n, preferred_element_type=F32)) / denom
    o_lat = o_lat.astype(BF16)
    for pair in range(A_HEADS // 2):
        h0, h1 = 2 * pair, 2 * pair + 1
        out = (jnp.dot(o_lat[h0 * L:(h0 + 1) * L], wuvs_ref[h0], preferred_element_type=F32)
               + jnp.dot(o_lat[h1 * L:(h1 + 1) * L], wuvs_ref[h1], preferred_element_type=F32))
        o_ref[:, pair * LANES:(pair + 1) * LANES] = out.astype(BF16)


def _attn_sample(qa, ckv_new, kr_new, cache_ckv, cache_krope_t, layer, prm, pick, name):
    T = qa.shape[0]
    n_streams, past = cache_ckv.shape[1], cache_ckv.shape[2]
    L = T // n_streams
    tok = lambda w: pl.BlockSpec((L, w), lambda b: (b, 0))
    new = lambda w: pl.BlockSpec((None, L, w), lambda b: (layer, b, 0))
    return pl.pallas_call(
        _attn_sample_kernel,
        grid=(n_streams,),
        in_specs=[tok(QK_WIDTH), new(KV_RANK), new(ROPE_DIM),
                  pl.BlockSpec((None, None, past, KV_RANK), lambda b: (layer, b, 0, 0)),
                  pl.BlockSpec((None, None, ROPE_DIM, past), lambda b: (layer, b, 0, 0)),
                  _resident((None, A_HEADS, HEAD_SLOT, KV_RANK), (layer, 0, 0, 0)),
                  _resident((None, A_HEADS, KV_RANK, LANES), (layer, 0, 0, 0)),
                  _resident((HEAD_SLOT, ROPE_DIM), (0, 0))],
        out_specs=tok(A_WIDTH),
        out_shape=jax.ShapeDtypeStruct((T, A_WIDTH), BF16),
        compiler_params=pltpu.CompilerParams(dimension_semantics=("arbitrary",), vmem_limit_bytes=VMEM_LIMIT),
        name=name,
    )(qa, ckv_new, kr_new, cache_ckv, cache_krope_t, prm["w_ukt"], prm["w_uvs"], pick)


def _post_mixer_kernel(x_ref, hn_ref, om_ref, at_ref, gh_ref, wo_ref, gpm_ref, gpf_ref, w1_ref, w2_ref, gpo_ref,
                       o_ref):
    tm = x_ref.shape[0]
    groups = [slice(g * tm // ROW_GROUPS, (g + 1) * tm // ROW_GROUPS) for g in range(ROW_GROUPS)]
    mix = []
    for r in groups:
        h_m = jax.nn.sigmoid(om_ref[r, :]) * (hn_ref[r, :] * gh_ref[...])
        mix.append(jnp.dot(jnp.concatenate([h_m.astype(BF16), at_ref[r, :]], axis=-1), wo_ref[...],
                           preferred_element_type=F32))
    x1 = [x_ref[r, :] + _rms(m, gpm_ref[...]) for r, m in zip(groups, mix)]
    hb = [_rms(x, gpf_ref[...]).astype(BF16) for x in x1]
    acc = [None] * ROW_GROUPS
    for lo, hi in FF_CHUNKS:
        gate_up = [(jnp.dot(h, w1_ref[:, lo:hi], preferred_element_type=F32),
                    jnp.dot(h, w1_ref[:, D_FF + lo:D_FF + hi], preferred_element_type=F32)) for h in hb]
        for g, (gate, up) in enumerate(gate_up):
            act = (gate * jax.nn.sigmoid(gate) * up).astype(BF16)
            part = jnp.dot(act, w2_ref[lo:hi, :], preferred_element_type=F32)
            acc[g] = part if acc[g] is None else acc[g] + part
    for r, x, a in zip(groups, x1, acc):
        o_ref[r, :] = x + _rms(a, gpo_ref[...])


def _post_mixer(x, hn, om, attn, layer, prm, tm, name):
    T = x.shape[0]
    row = lambda w: pl.BlockSpec((tm, w), lambda i: (i, 0))
    lyr = lambda shape: _resident((None,) + shape, (layer, 0, 0))
    return pl.pallas_call(
        _post_mixer_kernel,
        grid=(T // tm,),
        in_specs=[row(D_MODEL), row(M_WIDTH), row(M_WIDTH), row(A_WIDTH), lyr((1, M_WIDTH)),
                  lyr((D_MODEL, D_MODEL)), lyr((1, D_MODEL)), lyr((1, D_MODEL)), lyr((D_MODEL, 2 * D_FF)),
                  lyr((D_FF, D_MODEL)), lyr((1, D_MODEL))],
        out_specs=row(D_MODEL),
        out_shape=jax.ShapeDtypeStruct((T, D_MODEL), F32),
        compiler_params=pltpu.CompilerParams(dimension_semantics=("arbitrary",), vmem_limit_bytes=VMEM_LIMIT),
        name=name,
    )(x, hn, om, attn, prm["g_mlstm_head"], prm["w_out"], prm["g_post_mix"], prm["g_pre_ffn"], prm["w_ffn_in"],
      prm["w_ffn_out"], prm["g_post_ffn"])


def _prepare_params(g_pre_mix, g_post_mix, g_pre_ffn, g_post_ffn, w_in, b_gates, g_mlstm_head, g_q_norm, w_uq,
                    g_kv_norm, w_ukv, w_out, w_ffn_in, w_ffn_out):
    depth = w_in.shape[0]
    sizes = (M_HEADS * M_DK, M_HEADS * M_DK, M_WIDTH, M_WIDTH, 2 * M_HEADS, Q_RANK, KV_RANK, ROPE_DIM)
    offs = [0]
    for s in sizes:
        offs.append(offs[-1] + s)
    part = lambda i: w_in[:, :, offs[i]:offs[i + 1]]
    zeros = lambda w: jnp.zeros((depth, D_MODEL, w), w_in.dtype)
    w_in_r = jnp.concatenate(
        [part(0), part(1), part(2), part(3), part(5), part(6),
         zeros(ROPE_LANE0), part(7), zeros(LANES - ROPE_LANE0 - ROPE_DIM),
         part(4), zeros(LANES - 2 * M_HEADS)], axis=-1).astype(BF16)
    w_uq_r = jnp.pad(w_uq.reshape(depth, Q_RANK, A_HEADS, NOPE_DIM + ROPE_DIM),
                     ((0, 0), (0, 0), (0, 0), (0, HEAD_SLOT - NOPE_DIM - ROPE_DIM)))
    w_uq_r = w_uq_r.reshape(depth, Q_RANK, QK_WIDTH).astype(BF16)
    w_ukv4 = w_ukv.reshape(depth, KV_RANK, A_HEADS, NOPE_DIM + V_DIM)
    w_uk4 = jnp.pad(w_ukv4[..., :NOPE_DIM], ((0, 0), (0, 0), (0, 0), (0, HEAD_SLOT - NOPE_DIM)))
    w_uk_r = w_uk4.reshape(depth, KV_RANK, QK_WIDTH).astype(BF16)
    w_ukt = w_uk4.transpose(0, 2, 3, 1).astype(BF16)
    w_uv4 = w_ukv4[..., NOPE_DIM:]
    w_uvt = jnp.pad(w_uv4, ((0, 0), (0, 0), (0, 0), (0, V_SLOT - V_DIM)))
    w_uvt = w_uvt.reshape(depth, KV_RANK, VT_ROWS).transpose(0, 2, 1).astype(BF16)
    w_uvs = jnp.stack([jnp.pad(w_uv4[:, :, h, :], ((0, 0), (0, 0), ((h % 2) * V_DIM, (1 - h % 2) * V_DIM)))
                       for h in range(A_HEADS)], axis=1).astype(BF16)
    row = lambda g: g.astype(F32)[:, None, :]
    return {
        "g_pre_mix": row(g_pre_mix), "g_post_mix": row(g_post_mix), "g_pre_ffn": row(g_pre_ffn),
        "g_post_ffn": row(g_post_ffn), "g_mlstm_head": row(g_mlstm_head), "g_q_norm": row(g_q_norm),
        "g_kv_norm": row(g_kv_norm),
        "b_gates": jnp.pad(b_gates.astype(F32), ((0, 0), (0, LANES - 2 * M_HEADS)))[:, None, :],
        "w_in": w_in_r, "w_uq": w_uq_r, "w_uk": w_uk_r, "w_ukt": w_ukt, "w_uvt": w_uvt, "w_uvs": w_uvs,
        "w_out": w_out.astype(BF16), "w_ffn_in": w_ffn_in.astype(BF16), "w_ffn_out": w_ffn_out.astype(BF16),
    }


def _rope_tables(pos):
    inv = ROPE_THETA ** (-jnp.arange(ROPE_HALF, dtype=F32) / ROPE_HALF)
    ang = pos.astype(F32)[:, None] * inv[None, :]
    cos, sin = jnp.cos(ang), jnp.sin(ang)
    n = pos.shape[0]
    z = lambda w: jnp.zeros((n, w), F32)
    tail = LANES - ROPE_LANE0 - ROPE_DIM
    ra = jnp.concatenate([jnp.ones((n, ROPE_LANE0), F32), cos, cos, z(tail)], axis=-1)
    rbm = jnp.concatenate([z(ROPE_LANE0), -sin, z(ROPE_HALF), z(tail)], axis=-1)
    rbp = jnp.concatenate([z(ROPE_LANE0), z(ROPE_HALF), sin, z(tail)], axis=-1)
    return ra, rbm, rbp


def _rope_lane_picker():
    r = lax.broadcasted_iota(jnp.int32, (HEAD_SLOT, ROPE_DIM), 0)
    c = lax.broadcasted_iota(jnp.int32, (HEAD_SLOT, ROPE_DIM), 1)
    return (r == (c + ROPE_LANE0)).astype(BF16)


TM_PROMPT = 512
TM_SAMPLE = 256
MLSTM_SUB = 8
ROW_GROUPS = 2


def kernel(x_prompt, x_sample, cache_ckv, cache_krope, state_mlstm_C, state_mlstm_n, state_mlstm_m,
           g_pre_mix, g_post_mix, g_pre_ffn, g_post_ffn, w_in, b_gates, g_mlstm_head,
           g_q_norm, w_uq, g_kv_norm, w_ukv, w_out, w_ffn_in, w_ffn_out):
    B, S, _ = x_prompt.shape
    Bs, Ls, _ = x_sample.shape
    depth, _, past, _ = cache_ckv.shape
    assert S % TM_PROMPT == 0 and TM_PROMPT % KV_BLOCK == 0 and KV_BLOCK % CHUNK == 0
    assert TM_SAMPLE % Ls == 0 and TM_SAMPLE % KV_BLOCK == 0 and (Bs * Ls) % TM_SAMPLE == 0

    prm = _prepare_params(g_pre_mix, g_post_mix, g_pre_ffn, g_post_ffn, w_in, b_gates, g_mlstm_head, g_q_norm,
                          w_uq, g_kv_norm, w_ukv, w_out, w_ffn_in, w_ffn_out)
    rope_p = _rope_tables(jnp.arange(S))
    rope_s = _rope_tables(past + jnp.arange(TM_SAMPLE) % Ls)
    pick = _rope_lane_picker()
    cache_krope_t = jnp.swapaxes(cache_krope, 2, 3)

    xp = x_prompt.reshape(B * S, D_MODEL)
    xs = x_sample.reshape(Bs * Ls, D_MODEL)
    zero_st = jnp.zeros((B, PAIRS, LANES, ST_COLS), F32)
    zero_m = jnp.zeros((B, 1, M_WIDTH), F32)
    m_in = jnp.repeat(state_mlstm_m.astype(F32), M_DV, axis=-1)[:, :, None, :]

    ckv_p, kr_p = jnp.zeros((depth, B * S, KV_RANK), F32), jnp.zeros((depth, B * S, ROPE_DIM), F32)
    ckv_s, kr_s = jnp.zeros((depth, Bs * Ls, KV_RANK), F32), jnp.zeros((depth, Bs * Ls, ROPE_DIM), F32)

    st_in = _pack_state(state_mlstm_C.reshape((depth * Bs,) + state_mlstm_C.shape[2:]),
                        state_mlstm_n.reshape((depth * Bs,) + state_mlstm_n.shape[2:]))
    st_in = st_in.reshape((depth, Bs) + st_in.shape[1:])

    outs_p = {k: [] for k in ("st", "m")}
    outs_s = {k: [] for k in ("st", "m")}
    for l in range(depth):
        qm, km, vm, om, gates, qa, ka, vt, ckv_p, kr_p = _pre_mixer(
            xp, l, prm, rope_p, TM_PROMPT, S // TM_PROMPT, ckv_p, kr_p, f"pre_mixer_p{l}")
        hn, st, m = _mlstm(qm, km, vm, gates, zero_st, zero_m, B, S // CHUNK, CHUNK, MLSTM_SUB, f"mlstm_p{l}")
        attn = _attn_prompt(qa, ka, vt, B, S, f"attn_p{l}")
        xp = _post_mixer(xp, hn, om, attn, l, prm, TM_PROMPT, f"post_mixer_p{l}")
        outs_p["st"].append(st)
        outs_p["m"].append(m)
        qm, km, vm, om, gates, qa, _, _, ckv_s, kr_s = _pre_mixer(
            xs, l, prm, rope_s, TM_SAMPLE, 1, ckv_s, kr_s, f"pre_mixer_s{l}")
        hn, st, m = _mlstm(qm, km, vm, gates, st_in[l], m_in[l], Bs, 1, Ls, 1, f"mlstm_s{l}")
        attn = _attn_sample(qa, ckv_s, kr_s, cache_ckv, cache_krope_t, l, prm, pick, f"attn_s{l}")
        xs = _post_mixer(xs, hn, om, attn, l, prm, TM_SAMPLE, f"post_mixer_s{l}")
        outs_s["st"].append(st)
        outs_s["m"].append(m)

    def collect(ckv, kr, o, nb, ln):
        st = jnp.concatenate(o["st"], axis=0)
        C, n = _unpack_state(st)
        return (ckv.reshape(depth, nb, ln, KV_RANK), kr.reshape(depth, nb, ln, ROPE_DIM),
                C.reshape((depth, nb) + C.shape[1:]), n.reshape((depth, nb) + n.shape[1:]),
                jnp.stack(o["m"])[:, :, 0, ::M_DV])

    return ((xp.reshape(B, S, D_MODEL), xs.reshape(Bs, Ls, D_MODEL))
            + collect(ckv_p, kr_p, outs_p, B, S) + collect(ckv_s, kr_s, outs_s, Bs, Ls))
```
